```python
import numpy as np
import jax
import jax.numpy as jnp
from jax import lax

D_MODEL = 1024
BATCH = 1
SEQ = 16384
DEPTH = 4

HEAD_DIM = 64
N_MIXERS = 2
A_Q_HEADS = 16
A_KV_HEADS = 4
A_HALF_WINDOW = 128
B_GROUPS = ((128, 1), (512, 4), (2048, 16))
B_Q_HEADS = 8
B_KV_HEADS = 2
D_FF = -(-8 * D_MODEL // (3 * 256)) * 256
A_QKV = (A_Q_HEADS + 2 * A_KV_HEADS) * HEAD_DIM
B_QKV = len(B_GROUPS) * (B_Q_HEADS + 2 * B_KV_HEADS) * HEAD_DIM
A_OUT = A_Q_HEADS * HEAD_DIM
B_OUT = B_Q_HEADS * HEAD_DIM
N_A_LAYERS = (DEPTH + 1) // 2
N_B_LAYERS = DEPTH // 2
RMS_EPS = 1e-6
NEG = -1e30

kernel_name = "hybrid_window_dilated_alibi_encoder"


def rmsnorm(x, g):
    xf = x.astype(jnp.float32)
    y = xf * lax.rsqrt(jnp.mean(xf * xf, axis=-1, keepdims=True) + RMS_EPS)
    return (y * g.astype(jnp.float32)).astype(x.dtype)


def alibi_slopes(n):
    return jnp.asarray(2.0 ** (-8.0 * np.arange(1, n + 1) / n), dtype=jnp.float32)


def banded_attention(q, k, v, half_window, slopes, dist_scale, sink=None):
    n, L, hkv, g, dh = q.shape
    blk = half_window
    nb = -(-L // blk)
    lp = nb * blk
    qb = jnp.pad(q, ((0, 0), (0, lp - L), (0, 0), (0, 0), (0, 0))).reshape(n, nb, blk, hkv, g, dh)
    pad_kv = ((0, 0), (blk, lp - L + blk), (0, 0), (0, 0))
    kb = jnp.pad(k, pad_kv).reshape(n, nb + 2, blk, hkv, dh)
    vb = jnp.pad(v, pad_kv).reshape(n, nb + 2, blk, hkv, dh)
    kw = jnp.concatenate([kb[:, :-2], kb[:, 1:-1], kb[:, 2:]], axis=2)
    vw = jnp.concatenate([vb[:, :-2], vb[:, 1:-1], vb[:, 2:]], axis=2)
    rel = jnp.arange(3 * blk)[None, :] - blk - jnp.arange(blk)[:, None]
    kpos = jnp.arange(nb)[:, None] * blk - blk + jnp.arange(3 * blk)[None, :]
    mask = (jnp.abs(rel) <= half_window)[None] & ((kpos >= 0) & (kpos < L))[:, None, :]
    dist = (dist_scale * jnp.abs(rel)).astype(jnp.float32)
    bias = -slopes.astype(jnp.float32)[:, :, None, None] * dist
    s = jnp.einsum('nbqhgd,nbkhd->nbhgqk', qb, kw,
                   preferred_element_type=jnp.float32) * (dh ** -0.5) + bias
    s = jnp.where(mask[None, :, None, None], s, NEG)
    m = s.max(axis=-1)
    if sink is not None:
        sk = sink.astype(jnp.float32)[:, :, None]
        m = jnp.maximum(m, sk)
    p = jnp.exp(s - m[..., None])
    l = p.sum(axis=-1)
    if sink is not None:
        l = l + jnp.exp(sk - m)
    o = jnp.einsum('nbhgqk,nbkhd->nbqhgd', p, vw.astype(jnp.float32))
    o = o / jnp.moveaxis(l, -1, 2)[..., None]
    lse = jnp.moveaxis(m + jnp.log(l), -1, 2)
    o = o.reshape(n, lp, hkv, g, dh)[:, :L].astype(q.dtype)
    lse = lse.reshape(n, lp, hkv, g)[:, :L]
    return o, lse


def to_strided(t, dil):
    b, s = t.shape[:2]
    rest = t.shape[2:]
    t = t.reshape(b, s // dil, dil, *rest)
    return jnp.moveaxis(t, 2, 1).reshape(b * dil, s // dil, *rest)


def from_strided(t, b, dil):
    l = t.shape[1]
    rest = t.shape[2:]
    t = t.reshape(b, dil, l, *rest)
    return jnp.moveaxis(t, 1, 2).reshape(b, l * dil, *rest)


def windowed_gqa(h, w_in, w_out, sink):
    b, s, _ = h.shape
    gq = A_Q_HEADS // A_KV_HEADS
    dq, dk = A_Q_HEADS * HEAD_DIM, A_KV_HEADS * HEAD_DIM
    qkv = h @ w_in
    q = qkv[..., :dq].reshape(b, s, A_KV_HEADS, gq, HEAD_DIM)
    k = qkv[..., dq:dq + dk].reshape(b, s, A_KV_HEADS, HEAD_DIM)
    v = qkv[..., dq + dk:].reshape(b, s, A_KV_HEADS, HEAD_DIM)
    slopes = alibi_slopes(A_Q_HEADS).reshape(A_KV_HEADS, gq)
    o, _ = banded_attention(q, k, v, A_HALF_WINDOW, slopes, 1, sink.reshape(A_KV_HEADS, gq))
    return o.reshape(b, s, A_OUT) @ w_out


def dilated_attention(h, w_in, w_out):
    b, s, _ = h.shape
    ng = len(B_GROUPS)
    gq = B_Q_HEADS // B_KV_HEADS
    dq, dk = ng * B_Q_HEADS * HEAD_DIM, ng * B_KV_HEADS * HEAD_DIM
    qkv = h @ w_in
    q = qkv[..., :dq].reshape(b, s, ng, B_KV_HEADS, gq, HEAD_DIM)
    k = qkv[..., dq:dq + dk].reshape(b, s, ng, B_KV_HEADS, HEAD_DIM)
    v = qkv[..., dq + dk:].reshape(b, s, ng, B_KV_HEADS, HEAD_DIM)
    slopes = alibi_slopes(ng * B_Q_HEADS).reshape(ng, B_KV_HEADS, gq)
    outs, lses = [], []
    for gi, (window, dil) in enumerate(B_GROUPS):
        half = window // (2 * dil)
        o, lse = banded_attention(to_strided(q[:, :, gi], dil), to_strided(k[:, :, gi], dil),
                                  to_strided(v[:, :, gi], dil), half, slopes[gi], dil)
        outs.append(from_strided(o, b, dil))
        lses.append(from_strided(lse, b, dil))
    alpha = jax.nn.softmax(jnp.stack(lses), axis=0)
    o = jnp.einsum('nbshg,nbshgd->bshgd', alpha,
                   jnp.stack(outs).astype(jnp.float32)).astype(h.dtype)
    return o.reshape(b, s, B_OUT) @ w_out


def swiglu(h, w_in, w_out):
    gu = h @ w_in
    gate, up = jnp.split(gu, 2, axis=-1)
    return (jax.nn.silu(gate) * up) @ w_out


def setup_inputs(seed: int = 0) -> dict:
    key = jax.random.key(seed)
    ks = jax.random.split(key, 16)
    D = D_MODEL
    nrm = lambda k, shape, scale: jax.random.normal(k, shape, jnp.float32) * scale
    return {
        "x": nrm(ks[0], (BATCH, SEQ, D), 1.0),
        "c": nrm(ks[1], (BATCH, D), 1.0),
        "ada_w": nrm(ks[2], (DEPTH, D, 6 * D), 0.5 * D ** -0.5),
        "ada_b": nrm(ks[3], (DEPTH, 6 * D), 0.1),
        "norm_mix": 1.0 + nrm(ks[4], (DEPTH, D), 0.05),
        "norm_ffn": 1.0 + nrm(ks[5], (DEPTH, D), 0.05),
        "ffn_w_in": nrm(ks[6], (DEPTH, D, 2 * D_FF), D ** -0.5),
        "ffn_w_out": nrm(ks[7], (DEPTH, D_FF, D), D_FF ** -0.5),
        "a_w_in": nrm(ks[8], (N_A_LAYERS, D, A_QKV), D ** -0.5),
        "a_w_out": nrm(ks[9], (N_A_LAYERS, A_OUT, D), A_OUT ** -0.5),
        "a_sink": nrm(ks[10], (N_A_LAYERS, A_Q_HEADS), 0.5),
        "b_w_in": nrm(ks[11], (N_B_LAYERS, D, B_QKV), D ** -0.5),
        "b_w_out": nrm(ks[12], (N_B_LAYERS, B_OUT, D), B_OUT ** -0.5),
        "final_norm": 1.0 + nrm(ks[13], (D,), 0.05),
    }


def reference(x, c, ada_w, ada_b, norm_mix, norm_ffn, ffn_w_in, ffn_w_out,
              a_w_in, a_w_out, a_sink, b_w_in, b_w_out, final_norm):
    cond = jax.nn.silu(c)
    for i in range(DEPTH):
        mod = (cond @ ada_w[i] + ada_b[i])[:, None, :]
        sh1, sc1, g1, sh2, sc2, g2 = jnp.split(mod, 6, axis=-1)
        h = rmsnorm(x, norm_mix[i]) * (1 + sc1) + sh1
        j = i // N_MIXERS
        if i % N_MIXERS == 0:
            y = windowed_gqa(h, a_w_in[j], a_w_out[j], a_sink[j])
        else:
            y = dilated_attention(h, b_w_in[j], b_w_out[j])
        x = x + g1 * y
        h = rmsnorm(x, norm_ffn[i]) * (1 + sc2) + sh2
        x = x + g2 * swiglu(h, ffn_w_in[i], ffn_w_out[i])
    return rmsnorm(x, final_norm)
```

```python
import functools

import numpy as np
import jax
import jax.numpy as jnp
from jax import lax
from jax.experimental import pallas as pl
from jax.experimental.pallas import tpu as pltpu

HEAD_DIM = 64
A_Q_HEADS = 16
A_KV_HEADS = 4
A_HALF_WINDOW = 128
B_GROUPS = ((128, 1), (512, 4), (2048, 16))
B_Q_HEADS = 8
B_KV_HEADS = 2
GQ = 4
RMS_EPS = 1e-6
NEG = -1e30

LANES = 128
QBLK = 128
VMEM_LIMIT = 56 * 1024 * 1024

BF16 = jnp.bfloat16
F32 = jnp.float32


def _alibi_slopes(n):
    return np.asarray(2.0 ** (-8.0 * np.arange(1, n + 1) / n), dtype=np.float32)


def _const_spec(shape):
    nd = len(shape)
    return pl.BlockSpec(shape, lambda *_: (0,) * nd, pipeline_mode=pl.Buffered(1))


def _params(n_axes=1):
    return pltpu.CompilerParams(
        dimension_semantics=("arbitrary",) * n_axes, vmem_limit_bytes=VMEM_LIMIT)


def _ada_kernel(c_ref, w_ref, b_ref, o_ref):
    c = c_ref[...]
    cond = c * jax.nn.sigmoid(c)
    o_ref[0] = jnp.sum(w_ref[0] * cond, axis=0, keepdims=True) + b_ref[0]


def _ada_mod(c, ada_w, ada_b):
    depth, d, n = ada_w.shape
    tn = 1536
    return pl.pallas_call(
        _ada_kernel,
        out_shape=jax.ShapeDtypeStruct((depth, 1, n), F32),
        grid=(depth, n // tn),
        in_specs=[
            pl.BlockSpec((d, 1), lambda i, j: (0, 0)),
            pl.BlockSpec((1, d, tn), lambda i, j: (i, 0, j)),
            pl.BlockSpec((1, 1, tn), lambda i, j: (i, 0, j)),
        ],
        out_specs=pl.BlockSpec((1, 1, tn), lambda i, j: (i, 0, j)),
        compiler_params=_params(2),
        name="ada_mod",
    )(c.reshape(d, 1), ada_w, ada_b.reshape(depth, 1, n))


def _norm_mod(x, g, shift, scale):
    ms = jnp.mean(x * x, axis=-1, keepdims=True)
    y = x * lax.rsqrt(ms + RMS_EPS)
    return (y * g) * (1.0 + scale) + shift


def _qkv_a_kernel(x_ref, mod_ref, g_ref, w_ref, q_ref, kv_ref):
    h = _norm_mod(x_ref[...], g_ref[...], mod_ref[0:1, :], mod_ref[1:2, :])
    r = jnp.dot(h.astype(BF16), w_ref[...], preferred_element_type=F32)
    dq = q_ref.shape[1]
    q_ref[...] = (r[:, :dq] * (HEAD_DIM ** -0.5)).astype(BF16)
    kv_ref[...] = r[:, dq:].astype(BF16)


def _qkv_a(x, mod, g, w):
    s, d = x.shape
    n = w.shape[1]
    dq = A_Q_HEADS * HEAD_DIM
    tm = 1024
    return pl.pallas_call(
        _qkv_a_kernel,
        out_shape=(jax.ShapeDtypeStruct((s, dq), BF16),
                   jax.ShapeDtypeStruct((s, n - dq), BF16)),
        grid=(s // tm,),
        in_specs=[
            pl.BlockSpec((tm, d), lambda i: (i, 0)),
            _const_spec(mod.shape),
            _const_spec(g.shape),
            _const_spec(w.shape),
        ],
        out_specs=(pl.BlockSpec((tm, dq), lambda i: (i, 0)),
                   pl.BlockSpec((tm, n - dq), lambda i: (i, 0))),
        compiler_params=_params(),
        name="qkv_a",
    )(x, mod, g, w)


B_DQ = B_Q_HEADS * HEAD_DIM
B_DKV = 2 * B_KV_HEADS * HEAD_DIM
B_DG = B_DQ + B_DKV


def _qkv_b_kernel(x_ref, mod_ref, g_ref, w_ref, *refs):
    out_refs, slab_ref = refs[:-1], refs[-1]
    tm = x_ref.shape[0]
    h = _norm_mod(x_ref[...], g_ref[...], mod_ref[0:1, :], mod_ref[1:2, :])
    hb = h.astype(BF16)
    nq = B_DQ // LANES
    for gi, (_, dil) in enumerate(B_GROUPS):
        q_ref, kv_ref = out_refs[2 * gi], out_refs[2 * gi + 1]
        r = jnp.dot(hb, w_ref[:, gi * B_DG:(gi + 1) * B_DG], preferred_element_type=F32)
        rq = r[:, :B_DQ] * (HEAD_DIM ** -0.5)
        rkv = r[:, B_DQ:]
        if dil == 1:
            q_ref[0] = rq.astype(BF16)
            kv_ref[0] = rkv.astype(BF16)
            continue
        for c in range(B_DG // LANES):
            src = rq if c < nq else rkv
            cc = c if c < nq else c - nq
            slab_ref[c] = src[:, cc * LANES:(cc + 1) * LANES]
        rows = tm // dil
        for rr in range(dil):
            for c in range(B_DG // LANES):
                v = slab_ref[c, pl.ds(rr, rows, stride=dil), :].astype(BF16)
                if c < nq:
                    q_ref[rr, :, c * LANES:(c + 1) * LANES] = v
                else:
                    kv_ref[rr, :, (c - nq) * LANES:(c - nq + 1) * LANES] = v


def _qkv_b(x, mod, g, w):
    s, d = x.shape
    tm = 1024
    out_shape, out_specs = [], []
    for _, dil in B_GROUPS:
        for width in (B_DQ, B_DKV):
            out_shape.append(jax.ShapeDtypeStruct((dil, s // dil, width), BF16))
            out_specs.append(pl.BlockSpec((dil, tm // dil, width), lambda i: (0, i, 0)))
    return pl.pallas_call(
        _qkv_b_kernel,
        out_shape=tuple(out_shape),
        grid=(s // tm,),
        in_specs=[
            pl.BlockSpec((tm, d), lambda i: (i, 0)),
            _const_spec(mod.shape),
            _const_spec(g.shape),
            _const_spec(w.shape),
        ],
        out_specs=tuple(out_specs),
        scratch_shapes=[pltpu.VMEM((B_DG // LANES, tm, LANES), F32)],
        compiler_params=_params(),
        name="qkv_b",
    )(x, mod, g, w)


def _attn_kernel(*refs, n_kv, halo, seq_len, dist_scale, slopes, has_sink, emit_lse):
    refs = list(refs)
    q_ref, kvp_ref, kvc_ref, kvn_ref = refs[:4]
    pos = 4
    sink_ref = None
    if has_sink:
        sink_ref = refs[pos]
        pos += 1
    o_ref = refs[pos]
    pos += 1
    lse_ref = None
    if emit_lse:
        lse_ref = refs[pos]
        pos += 1
    kvbuf, bias_ref = refs[pos], refs[pos + 1]

    ts = q_ref.shape[0]
    w = QBLK + 2 * halo
    step = pl.program_id(0)

    @pl.when(step == 0)
    def _():
        qi = lax.broadcasted_iota(jnp.int32, (QBLK, w), 0)
        kj = lax.broadcasted_iota(jnp.int32, (QBLK, w), 1)
        rel = jnp.abs(kj - halo - qi)
        dist = (dist_scale * rel).astype(F32)
        for h, slope in enumerate(slopes):
            bias_ref[h] = jnp.where(rel <= halo, (-float(slope)) * dist, NEG)

    kvbuf[0:halo, :] = kvp_ref[...]
    kvbuf[halo:halo + ts, :] = kvc_ref[...]
    kvbuf[halo + ts:, :] = kvn_ref[...]

    col = lax.broadcasted_iota(jnp.int32, (1, w), 1)
    dkv = n_kv * HEAD_DIM

    def body(qb, carry):
        r0 = pl.multiple_of(qb * QBLK, QBLK)
        seq_pos = (step * ts + r0) & (seq_len - 1)
        prev_neg = jnp.where(seq_pos == 0, NEG, 0.0).astype(F32)
        next_neg = jnp.where(seq_pos + QBLK == seq_len, NEG, 0.0).astype(F32)
        vbias = jnp.where(col < halo, prev_neg, jnp.where(col >= halo + QBLK, next_neg, 0.0))
        for kvh in range(n_kv):
            c0 = kvh * GQ * HEAD_DIM
            qg = q_ref[pl.ds(r0, QBLK), c0:c0 + GQ * HEAD_DIM]
            qs = jnp.concatenate(
                [qg[:, g * HEAD_DIM:(g + 1) * HEAD_DIM] for g in range(GQ)], axis=0)
            k = kvbuf[pl.ds(r0, w), kvh * HEAD_DIM:(kvh + 1) * HEAD_DIM]
            v = kvbuf[pl.ds(r0, w), dkv + kvh * HEAD_DIM:dkv + (kvh + 1) * HEAD_DIM]
            s = lax.dot_general(qs, k, (((1,), (1,)), ((), ())),
                                preferred_element_type=F32)
            ps, ls, ms = [], [], []
            for g in range(GQ):
                h = kvh * GQ + g
                sg = s[g * QBLK:(g + 1) * QBLK, :] + bias_ref[h] + vbias
                m = jnp.max(sg, axis=-1, keepdims=True)
                if has_sink:
                    m = jnp.maximum(m, sink_ref[h])
                p = jnp.exp(sg - m)
                l = jnp.sum(p, axis=-1, keepdims=True)
                if has_sink:
                    l = l + jnp.exp(sink_ref[h] - m)
                ps.append(p.astype(BF16))
                ls.append(l)
                ms.append(m)
            pv = jnp.dot(jnp.concatenate(ps, axis=0), v, preferred_element_type=F32)
            outs, lses = [], []
            for g in range(GQ):
                outs.append(pv[g * QBLK:(g + 1) * QBLK, :] / ls[g])
                if emit_lse:
                    lses.append(jnp.broadcast_to(ms[g] + jnp.log(ls[g]), (QBLK, HEAD_DIM)))
            o_ref[pl.ds(r0, QBLK), c0:c0 + GQ * HEAD_DIM] = (
                jnp.concatenate(outs, axis=1).astype(o_ref.dtype))
            if emit_lse:
                lse_ref[pl.ds(r0, QBLK), c0:c0 + GQ * HEAD_DIM] = jnp.concatenate(lses, axis=1)
        return carry

    lax.fori_loop(0, ts // QBLK, body, 0)


def _attention(q, kv, *, n_kv, halo, seq_len, dist_scale, slopes, sink=None,
               emit_lse=False, out_dtype=BF16):
    s, dq = q.shape
    dkv2 = kv.shape[1]
    ts = 1024
    assert seq_len % ts == 0 and ts % halo == 0 and dq == n_kv * GQ * HEAD_DIM
    nh_blocks = s // halo
    per = ts // halo
    in_specs = [
        pl.BlockSpec((ts, dq), lambda i: (i, 0)),
        pl.BlockSpec((halo, dkv2), lambda i: (jnp.maximum(i * per - 1, 0), 0)),
        pl.BlockSpec((ts, dkv2), lambda i: (i, 0)),
        pl.BlockSpec((halo, dkv2), lambda i: (jnp.minimum((i + 1) * per, nh_blocks - 1), 0)),
    ]
    args = [q, kv, kv, kv]
    if sink is not None:
        in_specs.append(pl.BlockSpec(memory_space=pltpu.SMEM))
        args.append(sink)
    out_shape = [jax.ShapeDtypeStruct((s, dq), out_dtype)]
    out_specs = [pl.BlockSpec((ts, dq), lambda i: (i, 0))]
    if emit_lse:
        out_shape.append(jax.ShapeDtypeStruct((s, dq), F32))
        out_specs.append(pl.BlockSpec((ts, dq), lambda i: (i, 0)))
    kern = functools.partial(
        _attn_kernel, n_kv=n_kv, halo=halo, seq_len=seq_len, dist_scale=dist_scale,
        slopes=tuple(float(x) for x in slopes), has_sink=sink is not None, emit_lse=emit_lse)
    return pl.pallas_call(
        kern,
        out_shape=tuple(out_shape),
        grid=(s // ts,),
        in_specs=in_specs,
        out_specs=tuple(out_specs),
        scratch_shapes=[
            pltpu.VMEM((ts + 2 * halo, dkv2), BF16),
            pltpu.VMEM((n_kv * GQ, QBLK, QBLK + 2 * halo), F32),
        ],
        compiler_params=_params(),
        name="banded_attn",
    )(*args)


FFN_SPLIT = 1536


def _silu(x):
    return x * jax.nn.sigmoid(x)


def _post_tail(o_bf16, x_ref, mod_ref, g_ref, wo_ref, wi_ref, wo2_ref, fin_ref, out_ref):
    x = x_ref[...]
    y = jnp.dot(o_bf16, wo_ref[...], preferred_element_type=F32)
    x1 = x + mod_ref[2:3, :] * y
    h2 = _norm_mod(x1, g_ref[...], mod_ref[3:4, :], mod_ref[4:5, :]).astype(BF16)
    dff = wo2_ref.shape[0]
    acc = None
    for lo, hi in ((0, FFN_SPLIT), (FFN_SPLIT, dff)):
        gate = jnp.dot(h2, wi_ref[:, lo:hi], preferred_element_type=F32)
        up = jnp.dot(h2, wi_ref[:, dff + lo:dff + hi], preferred_element_type=F32)
        act = (_silu(gate) * up).astype(BF16)
        part = jnp.dot(act, wo2_ref[lo:hi, :], preferred_element_type=F32)
        acc = part if acc is None else acc + part
    x2 = x1 + mod_ref[5:6, :] * acc
    if fin_ref is not None:
        ms = jnp.mean(x2 * x2, axis=-1, keepdims=True)
        x2 = (x2 * lax.rsqrt(ms + RMS_EPS)) * fin_ref[...]
    out_ref[...] = x2


def _post_a_kernel(*refs, final):
    o_ref, x_ref, mod_ref, g_ref, wo_ref, wi_ref, wo2_ref = refs[:7]
    fin_ref = refs[7] if final else None
    out_ref = refs[8] if final else refs[7]
    _post_tail(o_ref[...], x_ref, mod_ref, g_ref, wo_ref, wi_ref, wo2_ref, fin_ref, out_ref)


def _post_b_kernel(*refs, final):
    (o0_ref, l0_ref, o1_ref, l1_ref, o2_ref, l2_ref,
     x_ref, mod_ref, g_ref, wo_ref, wi_ref, wo2_ref) = refs[:12]
    pos = 12
    fin_ref = None
    if final:
        fin_ref = refs[pos]
        pos += 1
    out_ref = refs[pos]
    so1, sl1, so2, sl2, om_ref = refs[pos + 1:pos + 6]
    tm = x_ref.shape[0]
    nc = B_DQ // LANES
    for src_o, src_l, dst_o, dst_l in ((o1_ref, l1_ref, so1, sl1), (o2_ref, l2_ref, so2, sl2)):
        dil = src_o.shape[0]
        rows = tm // dil
        for rr in range(dil):
            for c in range(nc):
                dst_o[c, pl.ds(rr, rows, stride=dil), :] = src_o[rr, :, c * LANES:(c + 1) * LANES]
                dst_l[c, pl.ds(rr, rows, stride=dil), :] = src_l[rr, :, c * LANES:(c + 1) * LANES]
    for c in range(nc):
        cs = slice(c * LANES, (c + 1) * LANES)
        l0, l1, l2 = l0_ref[0, :, cs], sl1[c], sl2[c]
        mx = jnp.maximum(jnp.maximum(l0, l1), l2)
        w0, w1, w2 = jnp.exp(l0 - mx), jnp.exp(l1 - mx), jnp.exp(l2 - mx)
        num = w0 * o0_ref[0, :, cs] + w1 * so1[c] + w2 * so2[c]
        om_ref[:, cs] = (num / (w0 + w1 + w2)).astype(BF16)
    _post_tail(om_ref[...], x_ref, mod_ref, g_ref, wo_ref, wi_ref, wo2_ref, fin_ref, out_ref)


def _post(o_parts, x, mod, g, wo, wi, wo2, final_g, *, dilated):
    s, d = x.shape
    tm = 512
    final = final_g is not None
    in_specs, args = [], []
    if dilated:
        for arr in o_parts:
            dil = arr.shape[0]
            in_specs.append(pl.BlockSpec((dil, tm // dil, arr.shape[2]), lambda i: (0, i, 0)))
            args.append(arr)
    else:
        (o,) = o_parts
        in_specs.append(pl.BlockSpec((tm, o.shape[1]), lambda i: (i, 0)))
        args.append(o)
    in_specs += [pl.BlockSpec((tm, d), lambda i: (i, 0)), _const_spec(mod.shape),
                 _const_spec(g.shape), _const_spec(wo.shape), _const_spec(wi.shape),
                 _const_spec(wo2.shape)]
    args += [x, mod, g, wo, wi, wo2]
    if final:
        in_specs.append(_const_spec(final_g.shape))
        args.append(final_g)
    scratch = []
    if dilated:
        scratch = [pltpu.VMEM((B_DQ // LANES, tm, LANES), F32) for _ in range(4)]
        scratch.append(pltpu.VMEM((tm, B_DQ), BF16))
    kern = functools.partial(_post_b_kernel if dilated else _post_a_kernel, final=final)
    return pl.pallas_call(
        kern,
        out_shape=jax.ShapeDtypeStruct((s, d), F32),
        grid=(s // tm,),
        in_specs=in_specs,
        out_specs=pl.BlockSpec((tm, d), lambda i: (i, 0)),
        scratch_shapes=scratch,
        compiler_params=_params(),
        name="post_b" if dilated else "post_a",
    )(*args)


def _reorder_b_w_in(w):
    ng = len(B_GROUPS)
    dq, dk = ng * B_DQ, ng * B_KV_HEADS * HEAD_DIM
    dkg = B_KV_HEADS * HEAD_DIM
    parts = []
    for gi in range(ng):
        parts += [w[:, gi * B_DQ:(gi + 1) * B_DQ],
                  w[:, dq + gi * dkg:dq + (gi + 1) * dkg],
                  w[:, dq + dk + gi * dkg:dq + dk + (gi + 1) * dkg]]
    return jnp.concatenate(parts, axis=1)


def kernel(x, c, ada_w, ada_b, norm_mix, norm_ffn, ffn_w_in, ffn_w_out,
           a_w_in, a_w_out, a_sink, b_w_in, b_w_out, final_norm):
    b, s, d = x.shape
    assert b == 1
    depth = ada_w.shape[0]
    xs = x.reshape(s, d)
    mods = _ada_mod(c, ada_w, ada_b).reshape(depth, 6, d)
    slopes_a = _alibi_slopes(A_Q_HEADS)
    slopes_b = _alibi_slopes(len(B_GROUPS) * B_Q_HEADS)
    for i in range(depth):
        j = i // 2
        mod = mods[i]
        g_mix = norm_mix[i].reshape(1, d)
        g_ffn = norm_ffn[i].reshape(1, d)
        wi = ffn_w_in[i].astype(BF16)
        wo2 = ffn_w_out[i].astype(BF16)
        fin = final_norm.reshape(1, d) if i == depth - 1 else None
        if i % 2 == 0:
            q, kv = _qkv_a(xs, mod, g_mix, a_w_in[j].astype(BF16))
            (o,) = _attention(q, kv, n_kv=A_KV_HEADS, halo=A_HALF_WINDOW, seq_len=s,
                              dist_scale=1, slopes=slopes_a, sink=a_sink[j])
            xs = _post((o,), xs, mod, g_ffn, a_w_out[j].astype(BF16), wi, wo2, fin,
                       dilated=False)
        else:
            outs = _qkv_b(xs, mod, g_mix, _reorder_b_w_in(b_w_in[j]).astype(BF16))
            parts = []
            for gi, (window, dil) in enumerate(B_GROUPS):
                qg = outs[2 * gi].reshape(s, B_DQ)
                kvg = outs[2 * gi + 1].reshape(s, B_DKV)
                og, lg = _attention(
                    qg, kvg, n_kv=B_KV_HEADS, halo=window // (2 * dil), seq_len=s // dil,
                    dist_scale=dil, slopes=slopes_b[gi * B_Q_HEADS:(gi + 1) * B_Q_HEADS],
                    emit_lse=True, out_dtype=F32)
                parts += [og.reshape(dil, s // dil, B_DQ), lg.reshape(dil, s // dil, B_DQ)]
            xs = _post(parts, xs, mod, g_ffn, b_w_out[j].astype(BF16), wi, wo2, fin,
                       dilated=True)
    return xs.reshape(b, s, d)
```

```python
import functools

import numpy as np
import jax
import jax.numpy as jnp
from jax import lax
from jax.experimental import pallas as pl
from jax.experimental.pallas import tpu as pltpu

HEAD_DIM = 64
A_Q_HEADS = 16
A_KV_HEADS = 4
A_HALF_WINDOW = 128
B_GROUPS = ((128, 1), (512, 4), (2048, 16))
B_Q_HEADS = 8
B_KV_HEADS = 2
GQ = 4
RMS_EPS = 1e-6
NEG = -1e30
LOG2E = 1.4426950408889634
LN2 = 0.6931471805599453
Q_SCALE = HEAD_DIM ** -0.5 * LOG2E

LANES = 128
QBLK = 128
ATTN_UNITS_PER_STEP = 16
VMEM_LIMIT = 56 * 1024 * 1024

BF16 = jnp.bfloat16
F32 = jnp.float32


def _alibi_slopes(n):
    return np.asarray(2.0 ** (-8.0 * np.arange(1, n + 1) / n), dtype=np.float32)


def _const_spec(shape):
    nd = len(shape)
    return pl.BlockSpec(shape, lambda *_: (0,) * nd, pipeline_mode=pl.Buffered(1))


def _params(n_axes=1):
    return pltpu.CompilerParams(
        dimension_semantics=("arbitrary",) * n_axes, vmem_limit_bytes=VMEM_LIMIT)


def _ada_kernel(c_ref, w_ref, b_ref, o_ref):
    c = c_ref[...]
    cond = c * jax.nn.sigmoid(c)
    o_ref[0] = jnp.sum(w_ref[0] * cond, axis=0, keepdims=True) + b_ref[0]


def _ada_mod(c, ada_w, ada_b):
    depth, d, n = ada_w.shape
    tn = 1536
    return pl.pallas_call(
        _ada_kernel,
        out_shape=jax.ShapeDtypeStruct((depth, 1, n), F32),
        grid=(depth, n // tn),
        in_specs=[
            pl.BlockSpec((d, 1), lambda i, j: (0, 0)),
            pl.BlockSpec((1, d, tn), lambda i, j: (i, 0, j)),
            pl.BlockSpec((1, 1, tn), lambda i, j: (i, 0, j)),
        ],
        out_specs=pl.BlockSpec((1, 1, tn), lambda i, j: (i, 0, j)),
        compiler_params=_params(2),
        name="ada_mod",
    )(c.reshape(d, 1), ada_w, ada_b.reshape(depth, 1, n))


def _norm_mod(x, g, shift, scale):
    ms = jnp.mean(x * x, axis=-1, keepdims=True)
    y = x * lax.rsqrt(ms + RMS_EPS)
    return (y * g) * (1.0 + scale) + shift


def _qkv_a_kernel(x_ref, mod_ref, g_ref, w_ref, q_ref, kv_ref):
    h = _norm_mod(x_ref[...], g_ref[...], mod_ref[0:1, :], mod_ref[1:2, :])
    r = jnp.dot(h.astype(BF16), w_ref[...], preferred_element_type=F32)
    dq = q_ref.shape[1]
    q_ref[...] = (r[:, :dq] * Q_SCALE).astype(BF16)
    kv_ref[...] = r[:, dq:].astype(BF16)


def _qkv_a(x, mod, g, w):
    s, d = x.shape
    n = w.shape[1]
    dq = A_Q_HEADS * HEAD_DIM
    tm = 1024
    return pl.pallas_call(
        _qkv_a_kernel,
        out_shape=(jax.ShapeDtypeStruct((s, dq), BF16),
                   jax.ShapeDtypeStruct((s, n - dq), BF16)),
        grid=(s // tm,),
        in_specs=[
            pl.BlockSpec((tm, d), lambda i: (i, 0)),
            _const_spec(mod.shape),
            _const_spec(g.shape),
            _const_spec(w.shape),
        ],
        out_specs=(pl.BlockSpec((tm, dq), lambda i: (i, 0)),
                   pl.BlockSpec((tm, n - dq), lambda i: (i, 0))),
        compiler_params=_params(),
        name="qkv_a",
    )(x, mod, g, w)


B_DQ = B_Q_HEADS * HEAD_DIM
B_DKV = 2 * B_KV_HEADS * HEAD_DIM
B_DG = B_DQ + B_DKV


def _qkv_b_kernel(x_ref, mod_ref, g_ref, w_ref, *refs):
    out_refs, slab_ref = refs[:-1], refs[-1]
    tm = x_ref.shape[0]
    h = _norm_mod(x_ref[...], g_ref[...], mod_ref[0:1, :], mod_ref[1:2, :])
    hb = h.astype(BF16)
    nq = B_DQ // LANES
    for gi, (_, dil) in enumerate(B_GROUPS):
        q_ref, kv_ref = out_refs[2 * gi], out_refs[2 * gi + 1]
        r = jnp.dot(hb, w_ref[:, gi * B_DG:(gi + 1) * B_DG], preferred_element_type=F32)
        rq = r[:, :B_DQ] * Q_SCALE
        rkv = r[:, B_DQ:]
        if dil == 1:
            q_ref[0] = rq.astype(BF16)
            kv_ref[0] = rkv.astype(BF16)
            continue
        for c in range(B_DG // LANES):
            src = rq if c < nq else rkv
            cc = c if c < nq else c - nq
            slab_ref[c] = src[:, cc * LANES:(cc + 1) * LANES]
        rows = tm // dil
        for rr in range(dil):
            for c in range(B_DG // LANES):
                v = slab_ref[c, pl.ds(rr, rows, stride=dil), :].astype(BF16)
                if c < nq:
                    q_ref[rr, :, c * LANES:(c + 1) * LANES] = v
                else:
                    kv_ref[rr, :, (c - nq) * LANES:(c - nq + 1) * LANES] = v


def _qkv_b(x, mod, g, w):
    s, d = x.shape
    tm = 1024
    out_shape, out_specs = [], []
    for _, dil in B_GROUPS:
        for width in (B_DQ, B_DKV):
            out_shape.append(jax.ShapeDtypeStruct((dil, s // dil, width), BF16))
            out_specs.append(pl.BlockSpec((dil, tm // dil, width), lambda i: (0, i, 0)))
    return pl.pallas_call(
        _qkv_b_kernel,
        out_shape=tuple(out_shape),
        grid=(s // tm,),
        in_specs=[
            pl.BlockSpec((tm, d), lambda i: (i, 0)),
            _const_spec(mod.shape),
            _const_spec(g.shape),
            _const_spec(w.shape),
        ],
        out_specs=tuple(out_specs),
        scratch_shapes=[pltpu.VMEM((B_DG // LANES, tm, LANES), F32)],
        compiler_params=_params(),
        name="qkv_b",
    )(x, mod, g, w)


def _attn_kernel(*refs, n_kv, halo, seq_len, dist_scale, slopes, has_sink, emit_lse):
    refs = list(refs)
    q_ref, kvp_ref, kvc_ref, kvn_ref = refs[:4]
    pos = 4
    sink_ref = None
    if has_sink:
        sink_ref = refs[pos]
        pos += 1
    o_ref = refs[pos]
    pos += 1
    lse_ref = None
    if emit_lse:
        lse_ref = refs[pos]
        pos += 1
    kvbuf, bias_ref = refs[pos], refs[pos + 1]

    ts = q_ref.shape[0]
    w = QBLK + 2 * halo
    nh = n_kv * GQ
    step = pl.program_id(0)

    @pl.when(step == 0)
    def _():
        kj = lax.broadcasted_iota(jnp.int32, (w, QBLK), 0)
        qi = lax.broadcasted_iota(jnp.int32, (w, QBLK), 1)
        rel = jnp.abs(kj - halo - qi)
        dist = (dist_scale * rel).astype(F32)
        for h, slope in enumerate(slopes):
            bias_ref[h] = jnp.where(rel <= halo, ((-float(slope)) * dist) * LOG2E, NEG)
        bias_ref[nh] = jnp.full((w, QBLK), NEG, F32)

    kvbuf[0:halo, :] = kvp_ref[...]
    kvbuf[halo:halo + ts, :] = kvc_ref[...]
    kvbuf[halo + ts:, :] = kvn_ref[...]

    dkv = n_kv * HEAD_DIM
    ones = jnp.ones((w, HEAD_DIM), BF16)
    nt_dims = (((1,), (1,)), ((), ()))
    tn_dims = (((0,), (0,)), ((), ()))

    def scores(r0, kvh):
        c0 = kvh * GQ * HEAD_DIM
        qg = q_ref[pl.ds(r0, QBLK), c0:c0 + GQ * HEAD_DIM]
        qs = jnp.concatenate(
            [qg[:, g * HEAD_DIM:(g + 1) * HEAD_DIM] for g in range(GQ)], axis=0)
        k = kvbuf[pl.ds(r0, w), kvh * HEAD_DIM:(kvh + 1) * HEAD_DIM]
        return lax.dot_general(k, qs, nt_dims, preferred_element_type=F32)

    def softmax(st, r0, kvh):
        seq_pos = (step * ts + r0) & (seq_len - 1)
        prev_out = seq_pos == 0
        next_out = seq_pos + QBLK == seq_len
        ps, ms = [], []
        for g in range(GQ):
            h = kvh * GQ + g
            top = jnp.where(prev_out, nh, h)
            bot = jnp.where(next_out, nh, h)
            bias = jnp.concatenate([bias_ref[top, 0:halo, :],
                                    bias_ref[h, halo:halo + QBLK, :],
                                    bias_ref[bot, halo + QBLK:w, :]], axis=0)
            sg = st[:, g * QBLK:(g + 1) * QBLK] + bias
            m = jnp.max(sg, axis=0, keepdims=True)
            if has_sink:
                m = jnp.maximum(m, sink_ref[h] * LOG2E)
            ps.append(jnp.exp2(sg - m).astype(BF16))
            ms.append(m)
        return jnp.concatenate(ps, axis=1), ms

    def weighted_values(pt, ms, r0, kvh):
        c0 = kvh * GQ * HEAD_DIM
        v = kvbuf[pl.ds(r0, w), dkv + kvh * HEAD_DIM:dkv + (kvh + 1) * HEAD_DIM]
        v_ext = jnp.concatenate([v, ones], axis=1)
        ot = lax.dot_general(v_ext, pt, tn_dims, preferred_element_type=F32)
        outs, lses = [], []
        for g in range(GQ):
            h = kvh * GQ + g
            cs = slice(g * QBLK, (g + 1) * QBLK)
            l = ot[HEAD_DIM:HEAD_DIM + 1, cs]
            if has_sink:
                l = l + jnp.exp2(sink_ref[h] * LOG2E - ms[g])
            outs.append(ot[0:HEAD_DIM, cs] * (1.0 / l))
            if emit_lse:
                lses.append(jnp.broadcast_to(ms[g] * LN2 + jnp.log(l), (HEAD_DIM, QBLK)))
        for pair in range(GQ // 2):
            cols = slice(c0 + pair * LANES, c0 + (pair + 1) * LANES)
            o2 = jnp.concatenate(outs[2 * pair:2 * pair + 2], axis=0)
            o_ref[pl.ds(r0, QBLK), cols] = o2.T.astype(o_ref.dtype)
            if emit_lse:
                l2 = jnp.concatenate(lses[2 * pair:2 * pair + 2], axis=0)
                lse_ref[pl.ds(r0, QBLK), cols] = l2.T

    units = [(qb * QBLK, kvh) for qb in range(ts // QBLK) for kvh in range(n_kv)]
    st = scores(*units[0])
    for u, unit in enumerate(units):
        st_next = scores(*units[u + 1]) if u + 1 < len(units) else None
        pt, ms = softmax(st, *unit)
        weighted_values(pt, ms, *unit)
        st = st_next


def _attention(q, kv, *, n_kv, halo, seq_len, dist_scale, slopes, sink=None,
               emit_lse=False, out_dtype=BF16):
    s, dq = q.shape
    dkv2 = kv.shape[1]
    ts = QBLK * (ATTN_UNITS_PER_STEP // n_kv)
    assert seq_len % ts == 0 and ts % halo == 0 and dq == n_kv * GQ * HEAD_DIM
    nh_blocks = s // halo
    per = ts // halo
    in_specs = [
        pl.BlockSpec((ts, dq), lambda i: (i, 0)),
        pl.BlockSpec((halo, dkv2), lambda i: (jnp.maximum(i * per - 1, 0), 0)),
        pl.BlockSpec((ts, dkv2), lambda i: (i, 0)),
        pl.BlockSpec((halo, dkv2), lambda i: (jnp.minimum((i + 1) * per, nh_blocks - 1), 0)),
    ]
    args = [q, kv, kv, kv]
    if sink is not None:
        in_specs.append(pl.BlockSpec(memory_space=pltpu.SMEM))
        args.append(sink)
    out_shape = [jax.ShapeDtypeStruct((s, dq), out_dtype)]
    out_specs = [pl.BlockSpec((ts, dq), lambda i: (i, 0))]
    if emit_lse:
        out_shape.append(jax.ShapeDtypeStruct((s, dq), F32))
        out_specs.append(pl.BlockSpec((ts, dq), lambda i: (i, 0)))
    kern = functools.partial(
        _attn_kernel, n_kv=n_kv, halo=halo, seq_len=seq_len, dist_scale=dist_scale,
        slopes=tuple(float(x) for x in slopes), has_sink=sink is not None, emit_lse=emit_lse)
    return pl.pallas_call(
        kern,
        out_shape=tuple(out_shape),
        grid=(s // ts,),
        in_specs=in_specs,
        out_specs=tuple(out_specs),
        scratch_shapes=[
            pltpu.VMEM((ts + 2 * halo, dkv2), BF16),
            pltpu.VMEM((n_kv * GQ + 1, QBLK + 2 * halo, QBLK), F32),
        ],
        compiler_params=_params(),
        name="banded_attn",
    )(*args)


FFN_SPLIT = 1536


def _silu(x):
    return x * jax.nn.sigmoid(x)


def _post_tail(o_bf16, x_ref, mod_ref, g_ref, wo_ref, wi_ref, wo2_ref, fin_ref, out_ref):
    x = x_ref[...]
    y = jnp.dot(o_bf16, wo_ref[...], preferred_element_type=F32)
    x1 = x + mod_ref[2:3, :] * y
    h2 = _norm_mod(x1, g_ref[...], mod_ref[3:4, :], mod_ref[4:5, :]).astype(BF16)
    dff = wo2_ref.shape[0]
    acc = None
    for lo, hi in ((0, FFN_SPLIT), (FFN_SPLIT, dff)):
        gate = jnp.dot(h2, wi_ref[:, lo:hi], preferred_element_type=F32)
        up = jnp.dot(h2, wi_ref[:, dff + lo:dff + hi], preferred_element_type=F32)
        act = (_silu(gate) * up).astype(BF16)
        part = jnp.dot(act, wo2_ref[lo:hi, :], preferred_element_type=F32)
        acc = part if acc is None else acc + part
    x2 = x1 + mod_ref[5:6, :] * acc
    if fin_ref is not None:
        ms = jnp.mean(x2 * x2, axis=-1, keepdims=True)
        x2 = (x2 * lax.rsqrt(ms + RMS_EPS)) * fin_ref[...]
    out_ref[...] = x2


def _post_a_kernel(*refs, final):
    o_ref, x_ref, mod_ref, g_ref, wo_ref, wi_ref, wo2_ref = refs[:7]
    fin_ref = refs[7] if final else None
    out_ref = refs[8] if final else refs[7]
    _post_tail(o_ref[...], x_ref, mod_ref, g_ref, wo_ref, wi_ref, wo2_ref, fin_ref, out_ref)


def _post_b_kernel(*refs, final):
    (o0_ref, l0_ref, o1_ref, l1_ref, o2_ref, l2_ref,
     x_ref, mod_ref, g_ref, wo_ref, wi_ref, wo2_ref) = refs[:12]
    pos = 12
    fin_ref = None
    if final:
        fin_ref = refs[pos]
        pos += 1
    out_ref = refs[pos]
    so1, sl1, so2, sl2, om_ref = refs[pos + 1:pos + 6]
    tm = x_ref.shape[0]
    nc = B_DQ // LANES
    for src_o, src_l, dst_o, dst_l in ((o1_ref, l1_ref, so1, sl1), (o2_ref, l2_ref, so2, sl2)):
        dil = src_o.shape[0]
        rows = tm // dil
        for rr in range(dil):
            for c in range(nc):
                dst_o[c, pl.ds(rr, rows, stride=dil), :] = src_o[rr, :, c * LANES:(c + 1) * LANES]
                dst_l[c, pl.ds(rr, rows, stride=dil), :] = src_l[rr, :, c * LANES:(c + 1) * LANES]
    for c in range(nc):
        cs = slice(c * LANES, (c + 1) * LANES)
        l0, l1, l2 = l0_ref[0, :, cs], sl1[c], sl2[c]
        mx = jnp.maximum(jnp.maximum(l0, l1), l2)
        w0, w1, w2 = jnp.exp(l0 - mx), jnp.exp(l1 - mx), jnp.exp(l2 - mx)
        num = w0 * o0_ref[0, :, cs] + w1 * so1[c] + w2 * so2[c]
        om_ref[:, cs] = (num / (w0 + w1 + w2)).astype(BF16)
    _post_tail(om_ref[...], x_ref, mod_ref, g_ref, wo_ref, wi_ref, wo2_ref, fin_ref, out_ref)


def _post(o_parts, x, mod, g, wo, wi, wo2, final_g, *, dilated):
    s, d = x.shape
    tm = 512
    final = final_g is not None
    in_specs, args = [], []
    if dilated:
        for arr in o_parts:
            dil = arr.shape[0]
            in_specs.append(pl.BlockSpec((dil, tm // dil, arr.shape[2]), lambda i: (0, i, 0)))
            args.append(arr)
    else:
        (o,) = o_parts
        in_specs.append(pl.BlockSpec((tm, o.shape[1]), lambda i: (i, 0)))
        args.append(o)
    in_specs += [pl.BlockSpec((tm, d), lambda i: (i, 0)), _const_spec(mod.shape),
                 _const_spec(g.shape), _const_spec(wo.shape), _const_spec(wi.shape),
                 _const_spec(wo2.shape)]
    args += [x, mod, g, wo, wi, wo2]
    if final:
        in_specs.append(_const_spec(final_g.shape))
        args.append(final_g)
    scratch = []
    if dilated:
        scratch = [pltpu.VMEM((B_DQ // LANES, tm, LANES), F32) for _ in range(4)]
        scratch.append(pltpu.VMEM((tm, B_DQ), BF16))
    kern = functools.partial(_post_b_kernel if dilated else _post_a_kernel, final=final)
    return pl.pallas_call(
        kern,
        out_shape=jax.ShapeDtypeStruct((s, d), F32),
        grid=(s // tm,),
        in_specs=in_specs,
        out_specs=pl.BlockSpec((tm, d), lambda i: (i, 0)),
        scratch_shapes=scratch,
        compiler_params=_params(),
        name="post_b" if dilated else "post_a",
    )(*args)


def _reorder_b_w_in(w):
    ng = len(B_GROUPS)
    dq, dk = ng * B_DQ, ng * B_KV_HEADS * HEAD_DIM
    dkg = B_KV_HEADS * HEAD_DIM
    parts = []
    for gi in range(ng):
        parts += [w[:, gi * B_DQ:(gi + 1) * B_DQ],
                  w[:, dq + gi * dkg:dq + (gi + 1) * dkg],
                  w[:, dq + dk + gi * dkg:dq + dk + (gi + 1) * dkg]]
    return jnp.concatenate(parts, axis=1)


def kernel(x, c, ada_w, ada_b, norm_mix, norm_ffn, ffn_w_in, ffn_w_out,
           a_w_in, a_w_out, a_sink, b_w_in, b_w_out, final_norm):
    b, s, d = x.shape
    assert b == 1
    depth = ada_w.shape[0]
    xs = x.reshape(s, d)
    mods = _ada_mod(c, ada_w, ada_b).reshape(depth, 6, d)
    slopes_a = _alibi_slopes(A_Q_HEADS)
    slopes_b = _alibi_slopes(len(B_GROUPS) * B_Q_HEADS)
    for i in range(depth):
        j = i // 2
        mod = mods[i]
        g_mix = norm_mix[i].reshape(1, d)
        g_ffn = norm_ffn[i].reshape(1, d)
        wi = ffn_w_in[i].astype(BF16)
        wo2 = ffn_w_out[i].astype(BF16)
        fin = final_norm.reshape(1, d) if i == depth - 1 else None
        if i % 2 == 0:
            q, kv = _qkv_a(xs, mod, g_mix, a_w_in[j].astype(BF16))
            (o,) = _attention(q, kv, n_kv=A_KV_HEADS, halo=A_HALF_WINDOW, seq_len=s,
                              dist_scale=1, slopes=slopes_a, sink=a_sink[j])
            xs = _post((o,), xs, mod, g_ffn, a_w_out[j].astype(BF16), wi, wo2, fin,
                       dilated=False)
        else:
            outs = _qkv_b(xs, mod, g_mix, _reorder_b_w_in(b_w_in[j]).astype(BF16))
            parts = []
            for gi, (window, dil) in enumerate(B_GROUPS):
                qg = outs[2 * gi].reshape(s, B_DQ)
                kvg = outs[2 * gi + 1].reshape(s, B_DKV)
                og, lg = _attention(
                    qg, kvg, n_kv=B_KV_HEADS, halo=window // (2 * dil), seq_len=s // dil,
                    dist_scale=dil, slopes=slopes_b[gi * B_Q_HEADS:(gi + 1) * B_Q_HEADS],
                    emit_lse=True, out_dtype=F32)
                parts += [og.reshape(dil, s // dil, B_DQ), lg.reshape(dil, s // dil, B_DQ)]
            xs = _post(parts, xs, mod, g_ffn, b_w_out[j].astype(BF16), wi, wo2, fin,
                       dilated=True)
    return xs.reshape(b, s, d)
```

```python
import functools
from typing import NamedTuple

import numpy as np
import jax
import jax.numpy as jnp
from jax import lax
from jax.experimental import pallas as pl
from jax.experimental.pallas import tpu as pltpu

HEAD_DIM = 64
A_Q_HEADS = 16
A_KV_HEADS = 4
A_HALF_WINDOW = 128
B_GROUPS = ((128, 1), (512, 4), (2048, 16))
B_Q_HEADS = 8
B_KV_HEADS = 2
GQ = 4
RMS_EPS = 1e-6
NEG = -1e30
LOG2E = 1.4426950408889634
LN2 = 0.6931471805599453
Q_SCALE = HEAD_DIM ** -0.5 * LOG2E

LANES = 128
QBLK = 128
ATTN_UNITS_PER_STEP = 32
ONES_COLS = 32
HEADS_PER_PV = 4
ATTN_LOOKAHEAD = 1
VMEM_LIMIT = 56 * 1024 * 1024

BF16 = jnp.bfloat16
F32 = jnp.float32


def _alibi_slopes(n):
    return np.asarray(2.0 ** (-8.0 * np.arange(1, n + 1) / n), dtype=np.float32)


class _Layer(NamedTuple):
    arr: jax.Array
    idx: int


def _layer_spec(p):
    nd = p.arr.ndim
    return pl.BlockSpec((None,) + p.arr.shape[1:], lambda *_: (p.idx,) + (0,) * (nd - 1),
                        pipeline_mode=pl.Buffered(1))


def _params(n_axes=1):
    return pltpu.CompilerParams(
        dimension_semantics=("arbitrary",) * n_axes, vmem_limit_bytes=VMEM_LIMIT)


def _ada_block(c_ref, w_ref, b_ref, o_ref):
    c = c_ref[...]
    cond = c * jax.nn.sigmoid(c)
    o_ref[...] = jnp.sum(w_ref[...] * cond, axis=0, keepdims=True) + b_ref[...]


def _ada_specs(ada_w, layer, tn, col_block):
    d = ada_w.shape[1]
    return ([pl.BlockSpec((d, 1), lambda *i: (0, 0)),
             pl.BlockSpec((None, d, tn), lambda *i: (layer, 0, col_block(*i))),
             pl.BlockSpec((None, 1, tn), lambda *i: (layer, 0, col_block(*i)))],
            pl.BlockSpec((1, tn), lambda *i: (0, col_block(*i))))


def _ada_mod(c_col, ada_w, ada_b3, layer):
    n = ada_w.shape[2]
    tn = 1536
    in_specs, out_spec = _ada_specs(ada_w, layer, tn, lambda j: j)
    return pl.pallas_call(
        _ada_block,
        out_shape=jax.ShapeDtypeStruct((1, n), F32),
        grid=(n // tn,),
        in_specs=in_specs,
        out_specs=out_spec,
        compiler_params=_params(),
        name="ada_mod",
    )(c_col, ada_w, ada_b3)


def _norm_mod(x, g, shift, scale):
    ms = jnp.mean(x * x, axis=-1, keepdims=True)
    y = x * lax.rsqrt(ms + RMS_EPS)
    return (y * g) * (1.0 + scale) + shift


def _qkv_a_kernel(x_ref, mod_ref, g_ref, w_ref, q_ref, kv_ref):
    dq = q_ref.shape[1]
    rows = x_ref.shape[0] // QKV_ROW_SLICES
    for n in range(QKV_ROW_SLICES):
        rs = slice(n * rows, (n + 1) * rows)
        h = _norm_mod(x_ref[rs, :], g_ref[...], mod_ref[0:1, :], mod_ref[1:2, :])
        r = jnp.dot(h.astype(BF16), w_ref[...], preferred_element_type=F32)
        q_ref[rs, :] = (r[:, :dq] * Q_SCALE).astype(BF16)
        kv_ref[rs, :] = r[:, dq:].astype(BF16)


def _qkv_a(x, mod, g, w):
    s, d = x.shape
    n = w.arr.shape[2]
    dq = A_Q_HEADS * HEAD_DIM
    tm = 1024
    return pl.pallas_call(
        _qkv_a_kernel,
        out_shape=(jax.ShapeDtypeStruct((s, dq), BF16),
                   jax.ShapeDtypeStruct((s, n - dq), BF16)),
        grid=(s // tm,),
        in_specs=[
            pl.BlockSpec((tm, d), lambda i: (i, 0)),
            _layer_spec(mod),
            _layer_spec(g),
            _layer_spec(w),
        ],
        out_specs=(pl.BlockSpec((tm, dq), lambda i: (i, 0)),
                   pl.BlockSpec((tm, n - dq), lambda i: (i, 0))),
        compiler_params=_params(),
        name="qkv_a",
    )(x, mod.arr, g.arr, w.arr)


B_DQ = B_Q_HEADS * HEAD_DIM
B_DKV = 2 * B_KV_HEADS * HEAD_DIM
B_DG = B_DQ + B_DKV


def _qkv_b_kernel(x_ref, mod_ref, g_ref, w_ref, *refs):
    out_refs, slab_ref = refs[:-1], refs[-1]
    tm = x_ref.shape[0]
    h = _norm_mod(x_ref[...], g_ref[...], mod_ref[0:1, :], mod_ref[1:2, :])
    hb = h.astype(BF16)
    nq = B_DQ // LANES
    for gi, (_, dil) in enumerate(B_GROUPS):
        q_ref, kv_ref = out_refs[2 * gi], out_refs[2 * gi + 1]
        r = jnp.dot(hb, w_ref[:, gi * B_DG:(gi + 1) * B_DG], preferred_element_type=F32)
        rq = r[:, :B_DQ] * Q_SCALE
        rkv = r[:, B_DQ:]
        if dil == 1:
            q_ref[0] = rq.astype(BF16)
            kv_ref[0] = rkv.astype(BF16)
            continue
        for c in range(B_DG // LANES):
            src = rq if c < nq else rkv
            cc = c if c < nq else c - nq
            slab_ref[c] = src[:, cc * LANES:(cc + 1) * LANES]
        rows = tm // dil
        for rr in range(dil):
            for c in range(B_DG // LANES):
                v = slab_ref[c, pl.ds(rr, rows, stride=dil), :].astype(BF16)
                if c < nq:
                    q_ref[rr, :, c * LANES:(c + 1) * LANES] = v
                else:
                    kv_ref[rr, :, (c - nq) * LANES:(c - nq + 1) * LANES] = v


def _qkv_b(x, mod, g, w):
    s, d = x.shape
    tm = 1024
    out_shape, out_specs = [], []
    for _, dil in B_GROUPS:
        for width in (B_DQ, B_DKV):
            out_shape.append(jax.ShapeDtypeStruct((dil, s // dil, width), BF16))
            out_specs.append(pl.BlockSpec((dil, tm // dil, width), lambda i: (0, i, 0)))
    return pl.pallas_call(
        _qkv_b_kernel,
        out_shape=tuple(out_shape),
        grid=(s // tm,),
        in_specs=[
            pl.BlockSpec((tm, d), lambda i: (i, 0)),
            _layer_spec(mod),
            _layer_spec(g),
            _layer_spec(w),
        ],
        out_specs=tuple(out_specs),
        scratch_shapes=[pltpu.VMEM((B_DG // LANES, tm, LANES), F32)],
        compiler_params=_params(),
        name="qkv_b",
    )(x, mod.arr, g.arr, w.arr)


def _attn_kernel(*refs, n_kv, halo, seq_len, dist_scale, slopes, sink_row, emit_lse):
    refs = list(refs)
    q_ref, kvp_ref, kvc_ref, kvn_ref = refs[:4]
    pos = 4
    has_sink = sink_row is not None
    sink_ref = None
    if has_sink:
        sink_ref = refs[pos]
        pos += 1
    o_ref = refs[pos]
    pos += 1
    lse_ref = None
    if emit_lse:
        lse_ref = refs[pos]
        pos += 1
    kvbuf, bias_ref = refs[pos], refs[pos + 1]

    ts = q_ref.shape[0]
    w = QBLK + 2 * halo
    nh = n_kv * GQ
    step = pl.program_id(0)

    @pl.when(step == 0)
    def _():
        kj = lax.broadcasted_iota(jnp.int32, (w, QBLK), 0)
        qi = lax.broadcasted_iota(jnp.int32, (w, QBLK), 1)
        rel = jnp.abs(kj - halo - qi)
        dist = (dist_scale * rel).astype(F32)
        for h, slope in enumerate(slopes):
            bias_ref[h] = jnp.where(rel <= halo, ((-float(slope)) * dist) * LOG2E, NEG)
        bias_ref[nh] = jnp.full((w, QBLK), NEG, F32)

    kvbuf[0:halo, :] = kvp_ref[...]
    kvbuf[halo:halo + ts, :] = kvc_ref[...]
    kvbuf[halo + ts:, :] = kvn_ref[...]

    dkv = n_kv * HEAD_DIM
    ones = jnp.ones((w, ONES_COLS), BF16)
    nt_dims = (((1,), (1,)), ((), ()))
    tn_dims = (((0,), (0,)), ((), ()))

    def scores(r0, kvh):
        c0 = kvh * GQ * HEAD_DIM
        qg = q_ref[pl.ds(r0, QBLK), c0:c0 + GQ * HEAD_DIM]
        qs = jnp.concatenate(
            [qg[:, g * HEAD_DIM:(g + 1) * HEAD_DIM] for g in range(GQ)], axis=0)
        k = kvbuf[pl.ds(r0, w), kvh * HEAD_DIM:(kvh + 1) * HEAD_DIM]
        return lax.dot_general(k, qs, nt_dims, preferred_element_type=F32)

    def softmax(st, r0, kvh, gs):
        seq_pos = (step * ts + r0) & (seq_len - 1)
        prev_out = seq_pos == 0
        next_out = seq_pos + QBLK == seq_len
        ps, ms = [], []
        for g in gs:
            h = kvh * GQ + g
            top = jnp.where(prev_out, nh, h)
            bot = jnp.where(next_out, nh, h)
            bias = jnp.concatenate([bias_ref[top, 0:halo, :],
                                    bias_ref[h, halo:halo + QBLK, :],
                                    bias_ref[bot, halo + QBLK:w, :]], axis=0)
            sg = st[:, g * QBLK:(g + 1) * QBLK] + bias
            m = jnp.max(sg, axis=0, keepdims=True)
            if has_sink:
                m = jnp.maximum(m, sink_ref[sink_row, h] * LOG2E)
            ps.append(jnp.exp2(sg - m).astype(BF16))
            ms.append(m)
        return jnp.concatenate(ps, axis=1), ms

    def weighted_values(pt, ms, r0, kvh, gs):
        v = kvbuf[pl.ds(r0, w), dkv + kvh * HEAD_DIM:dkv + (kvh + 1) * HEAD_DIM]
        v_ext = jnp.concatenate([v, ones], axis=1)
        ot = lax.dot_general(v_ext, pt, tn_dims, preferred_element_type=F32)
        outs, lses = [], []
        for n, g in enumerate(gs):
            h = kvh * GQ + g
            cs = slice(n * QBLK, (n + 1) * QBLK)
            l = ot[HEAD_DIM:HEAD_DIM + 1, cs]
            if has_sink:
                l = l + jnp.exp2(sink_ref[sink_row, h] * LOG2E - ms[n])
            outs.append(ot[0:HEAD_DIM, cs] * (1.0 / l))
            if emit_lse:
                lses.append(jnp.broadcast_to(ms[n] * LN2 + jnp.log(l), (HEAD_DIM, QBLK)))
        for n in range(0, len(gs), 2):
            c0 = (kvh * GQ + gs[n]) * HEAD_DIM
            o2 = jnp.concatenate(outs[n:n + 2], axis=0)
            o_ref[pl.ds(r0, QBLK), c0:c0 + LANES] = o2.T.astype(o_ref.dtype)
            if emit_lse:
                l2 = jnp.concatenate(lses[n:n + 2], axis=0)
                lse_ref[pl.ds(r0, QBLK), c0:c0 + LANES] = l2.T

    units = [(qb * QBLK, kvh) for qb in range(ts // QBLK) for kvh in range(n_kv)]
    head_sets = [tuple(range(i, i + HEADS_PER_PV)) for i in range(0, GQ, HEADS_PER_PV)]
    pending = [scores(*unit) for unit in units[:ATTN_LOOKAHEAD]]
    for u, unit in enumerate(units):
        if u + ATTN_LOOKAHEAD < len(units):
            pending.append(scores(*units[u + ATTN_LOOKAHEAD]))
        st = pending.pop(0)
        for gs in head_sets:
            pt, ms = softmax(st, *unit, gs)
            weighted_values(pt, ms, *unit, gs)


def _attention(q, kv, *, n_kv, halo, seq_len, dist_scale, slopes, sink=None,
               emit_lse=False, out_dtype=BF16):
    s, dq = q.shape
    dkv2 = kv.shape[1]
    ts = min(QBLK * (ATTN_UNITS_PER_STEP // n_kv), seq_len)
    assert seq_len % ts == 0 and ts % halo == 0 and dq == n_kv * GQ * HEAD_DIM
    nh_blocks = s // halo
    per = ts // halo
    in_specs = [
        pl.BlockSpec((ts, dq), lambda i: (i, 0)),
        pl.BlockSpec((halo, dkv2), lambda i: (jnp.maximum(i * per - 1, 0), 0)),
        pl.BlockSpec((ts, dkv2), lambda i: (i, 0)),
        pl.BlockSpec((halo, dkv2), lambda i: (jnp.minimum((i + 1) * per, nh_blocks - 1), 0)),
    ]
    args = [q, kv, kv, kv]
    if sink is not None:
        in_specs.append(pl.BlockSpec(memory_space=pltpu.SMEM))
        args.append(sink.arr)
    out_shape = [jax.ShapeDtypeStruct((s, dq), out_dtype)]
    out_specs = [pl.BlockSpec((ts, dq), lambda i: (i, 0))]
    if emit_lse:
        out_shape.append(jax.ShapeDtypeStruct((s, dq), F32))
        out_specs.append(pl.BlockSpec((ts, dq), lambda i: (i, 0)))
    kern = functools.partial(
        _attn_kernel, n_kv=n_kv, halo=halo, seq_len=seq_len, dist_scale=dist_scale,
        slopes=tuple(float(x) for x in slopes),
        sink_row=None if sink is None else sink.idx, emit_lse=emit_lse)
    return pl.pallas_call(
        kern,
        out_shape=tuple(out_shape),
        grid=(s // ts,),
        in_specs=in_specs,
        out_specs=tuple(out_specs),
        scratch_shapes=[
            pltpu.VMEM((ts + 2 * halo, dkv2), BF16),
            pltpu.VMEM((n_kv * GQ + 1, QBLK + 2 * halo, QBLK), F32),
        ],
        compiler_params=_params(),
        name="banded_attn",
    )(*args)


FFN_SPLIT = 1536
QKV_ROW_SLICES = 4
POST_ROW_SLICES = 2


def _silu(x):
    return x * jax.nn.sigmoid(x)


def _post_tail(get_o, x_ref, mod_ref, g_ref, wo_ref, wi_ref, wo2_ref, fin_ref, out_ref):
    tm = x_ref.shape[0]
    rows = tm // POST_ROW_SLICES
    dff = wo2_ref.shape[0]
    chunks = ((0, FFN_SPLIT), (FFN_SPLIT, dff))

    def gate_up(st, c):
        lo, hi = chunks[c]
        st["gu"] = (jnp.dot(st["h2"], wi_ref[:, lo:hi], preferred_element_type=F32),
                    jnp.dot(st["h2"], wi_ref[:, dff + lo:dff + hi], preferred_element_type=F32))

    def down(st, c):
        lo, hi = chunks[c]
        gate, up = st["gu_prev"]
        act = (_silu(gate) * up).astype(BF16)
        part = jnp.dot(act, wo2_ref[lo:hi, :], preferred_element_type=F32)
        st["acc"] = part if c == 0 else st["acc"] + part

    def stage(st, k):
        if k == 0:
            st["y"] = jnp.dot(get_o(st["rs"]), wo_ref[...], preferred_element_type=F32)
        elif k == 1:
            st["x1"] = x_ref[st["rs"], :] + mod_ref[2:3, :] * st["y"]
            st["h2"] = _norm_mod(st["x1"], g_ref[...], mod_ref[3:4, :],
                                 mod_ref[4:5, :]).astype(BF16)
            gate_up(st, 0)
        elif k <= len(chunks):
            st["gu_prev"] = st["gu"]
            gate_up(st, k - 1)
            down(st, k - 2)
        else:
            st["gu_prev"] = st["gu"]
            down(st, len(chunks) - 1)
            x2 = st["x1"] + mod_ref[5:6, :] * st["acc"]
            if fin_ref is not None:
                ms = jnp.mean(x2 * x2, axis=-1, keepdims=True)
                x2 = (x2 * lax.rsqrt(ms + RMS_EPS)) * fin_ref[...]
            out_ref[st["rs"], :] = x2

    states = [{"rs": slice(n * rows, (n + 1) * rows)} for n in range(POST_ROW_SLICES)]
    for k in range(len(chunks) + 2):
        for st in states:
            stage(st, k)


def _merged_o(attn_refs, slabs, om_ref):
    o0_ref, l0_ref, o1_ref, l1_ref, o2_ref, l2_ref = attn_refs
    so1, sl1, so2, sl2 = slabs
    nc = B_DQ // LANES

    def merged(rs):
        for src_o, src_l, dst_o, dst_l in ((o1_ref, l1_ref, so1, sl1), (o2_ref, l2_ref, so2, sl2)):
            dil = src_o.shape[0]
            n_src = (rs.stop - rs.start) // dil
            src_rs = slice(rs.start // dil, rs.start // dil + n_src)
            for rr in range(dil):
                dst_rows = pl.ds(rs.start + rr, n_src, stride=dil)
                for c in range(nc):
                    cs = slice(c * LANES, (c + 1) * LANES)
                    dst_o[c, dst_rows, :] = src_o[rr, src_rs, cs]
                    dst_l[c, dst_rows, :] = src_l[rr, src_rs, cs]
        for c in range(nc):
            cs = slice(c * LANES, (c + 1) * LANES)
            l0, l1, l2 = l0_ref[0, rs, cs], sl1[c, rs, :], sl2[c, rs, :]
            mx = jnp.maximum(jnp.maximum(l0, l1), l2)
            w0, w1, w2 = jnp.exp(l0 - mx), jnp.exp(l1 - mx), jnp.exp(l2 - mx)
            num = w0 * o0_ref[0, rs, cs] + w1 * so1[c, rs, :] + w2 * so2[c, rs, :]
            om_ref[rs, cs] = (num / (w0 + w1 + w2)).astype(BF16)
        return om_ref[rs, :]

    return merged


class _Cast(NamedTuple):
    arr: jax.Array
    idx: int
    rows: int
    cols: tuple | None = None


def _post_kernel(*refs, n_attn, final, cast_cols, has_ada):
    refs = list(refs)
    attn_refs = refs[:n_attn]
    x_ref, mod_ref, g_ref, wo_ref, wi_ref, wo2_ref = refs[n_attn:n_attn + 6]
    pos = n_attn + 6
    fin_ref = None
    if final:
        fin_ref = refs[pos]
        pos += 1
    n_cast = len(cast_cols)
    cast_src = refs[pos:pos + n_cast]
    pos += n_cast
    ada_in = refs[pos:pos + 3] if has_ada else None
    pos += 3 if has_ada else 0
    out_ref = refs[pos]
    cast_dst = refs[pos + 1:pos + 1 + n_cast]
    pos += 1 + n_cast
    ada_out = refs[pos] if has_ada else None
    pos += 1 if has_ada else 0
    scratch = refs[pos:]

    for src, dst, cols in zip(cast_src, cast_dst, cast_cols):
        if cols is None:
            dst[...] = src[...].astype(BF16)
        else:
            at = 0
            for start, width in cols:
                dst[:, at:at + width] = src[:, start:start + width].astype(BF16)
                at += width
    if has_ada:
        _ada_block(*ada_in, ada_out)

    if n_attn == 1:
        (o_ref,) = attn_refs
        get_o = lambda rs: o_ref[rs, :]
    else:
        get_o = _merged_o(attn_refs, scratch[:4], scratch[4])
    _post_tail(get_o, x_ref, mod_ref, g_ref, wo_ref, wi_ref, wo2_ref, fin_ref, out_ref)


def _post(o_parts, x, mod, wo, g, wi, wo2, final_g, *, casts=(), next_ada=None):
    s, d = x.shape
    tm = 512
    steps = s // tm
    final = final_g is not None
    dilated = len(o_parts) > 1
    in_specs, args = [], []
    if dilated:
        for arr in o_parts:
            dil = arr.shape[0]
            in_specs.append(pl.BlockSpec((dil, tm // dil, arr.shape[2]), lambda i: (0, i, 0)))
            args.append(arr)
    else:
        (o,) = o_parts
        in_specs.append(pl.BlockSpec((tm, o.shape[1]), lambda i: (i, 0)))
        args.append(o)
    params = [mod, g, wo, wi, wo2] + ([final_g] if final else [])
    in_specs += [pl.BlockSpec((tm, d), lambda i: (i, 0))] + [_layer_spec(p) for p in params]
    args += [x] + [p.arr for p in params]
    out_shape = [jax.ShapeDtypeStruct((s, d), F32)]
    out_specs = [pl.BlockSpec((tm, d), lambda i: (i, 0))]
    for cst in casts:
        _, n_rows, n_cols = cst.arr.shape
        last = n_rows // cst.rows - 1
        assert n_rows % cst.rows == 0 and last < steps
        in_specs.append(pl.BlockSpec((None, cst.rows, n_cols),
                                     lambda i, cst=cst, last=last: (cst.idx, jnp.minimum(i, last), 0)))
        args.append(cst.arr)
        out_shape.append(jax.ShapeDtypeStruct((n_rows, n_cols), BF16))
        out_specs.append(pl.BlockSpec((cst.rows, n_cols),
                                      lambda i, last=last: (jnp.minimum(i, last), 0)))
    if next_ada is not None:
        c_col, ada_w, ada_b3, layer = next_ada
        n = ada_w.shape[2]
        tn = 256
        last = n // tn - 1
        assert last < steps
        ada_in, ada_out = _ada_specs(ada_w, layer, tn, lambda i: jnp.minimum(i, last))
        in_specs += ada_in
        args += [c_col, ada_w, ada_b3]
        out_shape.append(jax.ShapeDtypeStruct((1, n), F32))
        out_specs.append(ada_out)
    scratch = []
    if dilated:
        scratch = [pltpu.VMEM((B_DQ // LANES, tm, LANES), F32) for _ in range(4)]
        scratch.append(pltpu.VMEM((tm, B_DQ), BF16))
    kern = functools.partial(
        _post_kernel, n_attn=len(o_parts), final=final,
        cast_cols=tuple(cst.cols for cst in casts), has_ada=next_ada is not None)
    return pl.pallas_call(
        kern,
        out_shape=tuple(out_shape),
        grid=(steps,),
        in_specs=in_specs,
        out_specs=tuple(out_specs),
        scratch_shapes=scratch,
        compiler_params=_params(),
        name="post_b" if dilated else "post_a",
    )(*args)


def _b_w_in_cols():
    ng = len(B_GROUPS)
    dq, dk = ng * B_DQ, ng * B_KV_HEADS * HEAD_DIM
    dkg = B_KV_HEADS * HEAD_DIM
    cols = []
    for gi in range(ng):
        cols += [(gi * B_DQ, B_DQ), (dq + gi * dkg, dkg), (dq + dk + gi * dkg, dkg)]
    return tuple(cols)


def _bf16_layer(arr, idx, cols=None):
    w = arr[idx]
    if cols is not None:
        w = jnp.concatenate([w[:, st:st + wd] for st, wd in cols], axis=1)
    return _Layer(w.astype(BF16)[None], 0)


def kernel(x, c, ada_w, ada_b, norm_mix, norm_ffn, ffn_w_in, ffn_w_out,
           a_w_in, a_w_out, a_sink, b_w_in, b_w_out, final_norm):
    b, s, d = x.shape
    assert b == 1
    depth = ada_w.shape[0]
    xs = x.reshape(s, d)
    c_col = c.reshape(d, 1)
    ada_b3 = ada_b.reshape(depth, 1, -1)
    g_mix = norm_mix.reshape(depth, 1, d)
    g_ffn = norm_ffn.reshape(depth, 1, d)
    slopes_a = _alibi_slopes(A_Q_HEADS)
    slopes_b = _alibi_slopes(len(B_GROUPS) * B_Q_HEADS)
    b_cols = _b_w_in_cols()

    def mixer_params(i):
        return (a_w_in, a_w_out, i // 2, None) if i % 2 == 0 else (b_w_in, b_w_out, i // 2, b_cols)

    mod = _ada_mod(c_col, ada_w, ada_b3, 0)
    m_in, m_out, j, cols = mixer_params(0)
    w_in, w_out = _bf16_layer(m_in, j, cols), _bf16_layer(m_out, j)
    wi, wo2 = _bf16_layer(ffn_w_in, 0), _bf16_layer(ffn_w_out, 0)
    for i in range(depth):
        mod_l = _Layer(mod.reshape(1, 6, d), 0)
        if i % 2 == 0:
            q, kv = _qkv_a(xs, mod_l, _Layer(g_mix, i), w_in)
            o_parts = _attention(q, kv, n_kv=A_KV_HEADS, halo=A_HALF_WINDOW, seq_len=s,
                                 dist_scale=1, slopes=slopes_a, sink=_Layer(a_sink, i // 2))
        else:
            outs = _qkv_b(xs, mod_l, _Layer(g_mix, i), w_in)
            o_parts = []
            for gi, (window, dil) in enumerate(B_GROUPS):
                qg = outs[2 * gi].reshape(s, B_DQ)
                kvg = outs[2 * gi + 1].reshape(s, B_DKV)
                og, lg = _attention(
                    qg, kvg, n_kv=B_KV_HEADS, halo=window // (2 * dil), seq_len=s // dil,
                    dist_scale=dil, slopes=slopes_b[gi * B_Q_HEADS:(gi + 1) * B_Q_HEADS],
                    emit_lse=True, out_dtype=F32)
                o_parts += [og.reshape(dil, s // dil, B_DQ), lg.reshape(dil, s // dil, B_DQ)]
        last = i == depth - 1
        casts, next_ada = (), None
        if not last:
            m_in, m_out, j, cols = mixer_params(i + 1)
            casts = (_Cast(m_in, j, 32, cols), _Cast(m_out, j, m_out.shape[1] // 32),
                     _Cast(ffn_w_in, i + 1, 32), _Cast(ffn_w_out, i + 1, 176))
            next_ada = (c_col, ada_w, ada_b3, i + 1)
        fin = _Layer(final_norm.reshape(1, 1, d), 0) if last else None
        res = _post(o_parts, xs, mod_l, w_out, _Layer(g_ffn, i), wi, wo2, fin,
                    casts=casts, next_ada=next_ada)
        xs = res[0]
        if not last:
            w_in, w_out, wi, wo2 = (_Layer(r[None], 0) for r in res[1:5])
            mod = res[5]
    return xs.reshape(b, s, d)
```

```python
import functools
from typing import NamedTuple

import numpy as np
import jax
import jax.numpy as jnp
from jax import lax
from jax.experimental import pallas as pl
from jax.experimental.pallas import tpu as pltpu

HEAD_DIM = 64
A_Q_HEADS = 16
A_KV_HEADS = 4
A_HALF_WINDOW = 128
B_GROUPS = ((128, 1), (512, 4), (2048, 16))
B_Q_HEADS = 8
B_KV_HEADS = 2
GQ = 4
RMS_EPS = 1e-6
NEG = -1e30
LOG2E = 1.4426950408889634
LN2 = 0.6931471805599453
Q_SCALE = HEAD_DIM ** -0.5 * LOG2E

LANES = 128
QBLK = 128
ATTN_UNITS_PER_STEP = 32
ONES_COLS = 32
HEADS_PER_PV = 4
ATTN_LOOKAHEAD = 1
FFN_SPLIT = 1536
QKV_ROW_SLICES = 4
POST_ROW_SLICES = 2
VMEM_LIMIT = 56 * 1024 * 1024

BF16 = jnp.bfloat16
F32 = jnp.float32


def _alibi_slopes(n):
    return np.asarray(2.0 ** (-8.0 * np.arange(1, n + 1) / n), dtype=np.float32)


class _Layer(NamedTuple):
    arr: jax.Array
    idx: int


def _layer_spec(p):
    nd = p.arr.ndim
    return pl.BlockSpec((None,) + p.arr.shape[1:], lambda *_: (p.idx,) + (0,) * (nd - 1),
                        pipeline_mode=pl.Buffered(1))


def _params(n_axes=1, **kwargs):
    return pltpu.CompilerParams(
        dimension_semantics=("arbitrary",) * n_axes, vmem_limit_bytes=VMEM_LIMIT, **kwargs)


def _ada_block(c_ref, w_ref, b_ref, o_ref):
    c = c_ref[...]
    cond = c * jax.nn.sigmoid(c)
    o_ref[...] = jnp.sum(w_ref[...] * cond, axis=0, keepdims=True) + b_ref[...]


def _ada_specs(ada_w, layer, tn, col_block):
    d = ada_w.shape[1]
    return ([pl.BlockSpec((d, 1), lambda *i: (0, 0)),
             pl.BlockSpec((None, d, tn), lambda *i: (layer, 0, col_block(*i))),
             pl.BlockSpec((None, 1, tn), lambda *i: (layer, 0, col_block(*i)))],
            pl.BlockSpec((1, tn), lambda *i: (0, col_block(*i))))


def _ada_mod(c_col, ada_w, ada_b3, layer):
    n = ada_w.shape[2]
    tn = 1536
    in_specs, out_spec = _ada_specs(ada_w, layer, tn, lambda j: j)
    return pl.pallas_call(
        _ada_block,
        out_shape=jax.ShapeDtypeStruct((1, n), F32),
        grid=(n // tn,),
        in_specs=in_specs,
        out_specs=out_spec,
        compiler_params=_params(),
        name="ada_mod",
    )(c_col, ada_w, ada_b3)


def _norm_mod(x, g, shift, scale):
    ms = jnp.mean(x * x, axis=-1, keepdims=True)
    y = x * lax.rsqrt(ms + RMS_EPS)
    return (y * g) * (1.0 + scale) + shift


class _Cast(NamedTuple):
    arr: jax.Array
    idx: int
    rows: int
    cols: tuple | None = None


def _cast_specs(casts, steps):
    in_specs, args, out_shape, out_specs = [], [], [], []
    for cst in casts:
        _, n_rows, n_cols = cst.arr.shape
        last = n_rows // cst.rows - 1
        assert n_rows % cst.rows == 0 and last < steps
        in_specs.append(pl.BlockSpec((None, cst.rows, n_cols),
                                     lambda i, cst=cst, last=last: (cst.idx, jnp.minimum(i, last), 0)))
        args.append(cst.arr)
        out_shape.append(jax.ShapeDtypeStruct((n_rows, n_cols), BF16))
        out_specs.append(pl.BlockSpec((cst.rows, n_cols),
                                      lambda i, last=last: (jnp.minimum(i, last), 0)))
    return in_specs, args, out_shape, out_specs


def _run_casts(srcs, dsts, cast_cols):
    for src, dst, cols in zip(srcs, dsts, cast_cols):
        if cols is None:
            dst[...] = src[...].astype(BF16)
        else:
            at = 0
            for start, width in cols:
                dst[:, at:at + width] = src[:, start:start + width].astype(BF16)
                at += width


def _qkv_a_kernel(x_ref, mod_ref, g_ref, w_ref, *refs, cast_cols):
    n_cast = len(cast_cols)
    cast_src, (q_ref, kv_ref), cast_dst = refs[:n_cast], refs[n_cast:n_cast + 2], refs[n_cast + 2:]
    _run_casts(cast_src, cast_dst, cast_cols)
    dq = q_ref.shape[1]
    rows = x_ref.shape[0] // QKV_ROW_SLICES
    for n in range(QKV_ROW_SLICES):
        rs = slice(n * rows, (n + 1) * rows)
        h = _norm_mod(x_ref[rs, :], g_ref[...], mod_ref[0:1, :], mod_ref[1:2, :])
        r = jnp.dot(h.astype(BF16), w_ref[...], preferred_element_type=F32)
        q_ref[rs, :] = (r[:, :dq] * Q_SCALE).astype(BF16)
        kv_ref[rs, :] = r[:, dq:].astype(BF16)


def _qkv_a(x, mod, g, w, casts=()):
    s, d = x.shape
    dq = A_Q_HEADS * HEAD_DIM
    dkv = w.arr.shape[2] - dq
    tm = 1024
    cast_in, cast_args, cast_shape, cast_out = _cast_specs(casts, s // tm)
    return pl.pallas_call(
        functools.partial(_qkv_a_kernel, cast_cols=tuple(cst.cols for cst in casts)),
        out_shape=(jax.ShapeDtypeStruct((s, dq), BF16),
                   jax.ShapeDtypeStruct((s, dkv), BF16), *cast_shape),
        grid=(s // tm,),
        in_specs=[
            pl.BlockSpec((tm, d), lambda i: (i, 0)),
            _layer_spec(mod),
            _layer_spec(g),
            _layer_spec(w),
            *cast_in,
        ],
        out_specs=(pl.BlockSpec((tm, dq), lambda i: (i, 0)),
                   pl.BlockSpec((tm, dkv), lambda i: (i, 0)), *cast_out),
        compiler_params=_params(),
        name="qkv_a",
    )(x, mod.arr, g.arr, w.arr, *cast_args)


B_DQ = B_Q_HEADS * HEAD_DIM
B_DKV = 2 * B_KV_HEADS * HEAD_DIM
B_DG = B_DQ + B_DKV


def _qkv_b_kernel(x_ref, mod_ref, g_ref, w_ref, *refs):
    out_refs, slab_ref = refs[:-1], refs[-1]
    tm = x_ref.shape[0]
    h = _norm_mod(x_ref[...], g_ref[...], mod_ref[0:1, :], mod_ref[1:2, :])
    hb = h.astype(BF16)
    nq = B_DQ // LANES
    for gi, (_, dil) in enumerate(B_GROUPS):
        q_ref, kv_ref = out_refs[2 * gi], out_refs[2 * gi + 1]
        r = jnp.dot(hb, w_ref[:, gi * B_DG:(gi + 1) * B_DG], preferred_element_type=F32)
        rq = r[:, :B_DQ] * Q_SCALE
        rkv = r[:, B_DQ:]
        if dil == 1:
            q_ref[0] = rq.astype(BF16)
            kv_ref[0] = rkv.astype(BF16)
            continue
        for c in range(B_DG // LANES):
            src = rq if c < nq else rkv
            cc = c if c < nq else c - nq
            slab_ref[c] = src[:, cc * LANES:(cc + 1) * LANES]
        rows = tm // dil
        for rr in range(dil):
            for c in range(B_DG // LANES):
                v = slab_ref[c, pl.ds(rr, rows, stride=dil), :].astype(BF16)
                if c < nq:
                    q_ref[rr, :, c * LANES:(c + 1) * LANES] = v
                else:
                    kv_ref[rr, :, (c - nq) * LANES:(c - nq + 1) * LANES] = v


def _qkv_b(x, mod, g, w):
    s, d = x.shape
    tm = 1024
    out_shape, out_specs = [], []
    for _, dil in B_GROUPS:
        for width in (B_DQ, B_DKV):
            out_shape.append(jax.ShapeDtypeStruct((dil, s // dil, width), BF16))
            out_specs.append(pl.BlockSpec((dil, tm // dil, width), lambda i: (0, i, 0)))
    return pl.pallas_call(
        _qkv_b_kernel,
        out_shape=tuple(out_shape),
        grid=(s // tm,),
        in_specs=[
            pl.BlockSpec((tm, d), lambda i: (i, 0)),
            _layer_spec(mod),
            _layer_spec(g),
            _layer_spec(w),
        ],
        out_specs=tuple(out_specs),
        scratch_shapes=[pltpu.VMEM((B_DG // LANES, tm, LANES), F32)],
        compiler_params=_params(),
        name="qkv_b",
    )(x, mod.arr, g.arr, w.arr)


def _attn_kernel(*refs, n_kv, halo, seq_len, dist_scale, slopes, sink_row, emit_lse):
    refs = list(refs)
    q_ref, kvp_ref, kvc_ref, kvn_ref = refs[:4]
    pos = 4
    has_sink = sink_row is not None
    sink_ref = None
    if has_sink:
        sink_ref = refs[pos]
        pos += 1
    o_ref = refs[pos]
    pos += 1
    lse_ref = None
    if emit_lse:
        lse_ref = refs[pos]
        pos += 1
    kvbuf, bias_ref = refs[pos], refs[pos + 1]

    ts = q_ref.shape[0]
    w = QBLK + 2 * halo
    nh = n_kv * GQ
    step = pl.program_id(0)

    @pl.when(step == 0)
    def _():
        kj = lax.broadcasted_iota(jnp.int32, (w, QBLK), 0)
        qi = lax.broadcasted_iota(jnp.int32, (w, QBLK), 1)
        rel = jnp.abs(kj - halo - qi)
        dist = (dist_scale * rel).astype(F32)
        for h, slope in enumerate(slopes):
            bias_ref[h] = jnp.where(rel <= halo, ((-float(slope)) * dist) * LOG2E, NEG)
        bias_ref[nh] = jnp.full((w, QBLK), NEG, F32)

    kvbuf[0:halo, :] = kvp_ref[...]
    kvbuf[halo:halo + ts, :] = kvc_ref[...]
    kvbuf[halo + ts:, :] = kvn_ref[...]

    dkv = n_kv * HEAD_DIM
    ones = jnp.ones((w, ONES_COLS), BF16)
    nt_dims = (((1,), (1,)), ((), ()))
    tn_dims = (((0,), (0,)), ((), ()))

    def scores(r0, kvh):
        c0 = kvh * GQ * HEAD_DIM
        qg = q_ref[pl.ds(r0, QBLK), c0:c0 + GQ * HEAD_DIM]
        qs = jnp.concatenate(
            [qg[:, g * HEAD_DIM:(g + 1) * HEAD_DIM] for g in range(GQ)], axis=0)
        k = kvbuf[pl.ds(r0, w), kvh * HEAD_DIM:(kvh + 1) * HEAD_DIM]
        return lax.dot_general(k, qs, nt_dims, preferred_element_type=F32)

    def softmax(st, r0, kvh, gs):
        seq_pos = (step * ts + r0) & (seq_len - 1)
        prev_out = seq_pos == 0
        next_out = seq_pos + QBLK == seq_len
        ps, ms = [], []
        for g in gs:
            h = kvh * GQ + g
            top = jnp.where(prev_out, nh, h)
            bot = jnp.where(next_out, nh, h)
            bias = jnp.concatenate([bias_ref[top, 0:halo, :],
                                    bias_ref[h, halo:halo + QBLK, :],
                                    bias_ref[bot, halo + QBLK:w, :]], axis=0)
            sg = st[:, g * QBLK:(g + 1) * QBLK] + bias
            m = jnp.max(sg, axis=0, keepdims=True)
            if has_sink:
                m = jnp.maximum(m, sink_ref[sink_row, h] * LOG2E)
            ps.append(jnp.exp2(sg - m).astype(BF16))
            ms.append(m)
        return jnp.concatenate(ps, axis=1), ms

    def weighted_values(pt, ms, r0, kvh, gs):
        v = kvbuf[pl.ds(r0, w), dkv + kvh * HEAD_DIM:dkv + (kvh + 1) * HEAD_DIM]
        v_ext = jnp.concatenate([v, ones], axis=1)
        ot = lax.dot_general(v_ext, pt, tn_dims, preferred_element_type=F32)
        outs, lses = [], []
        for n, g in enumerate(gs):
            h = kvh * GQ + g
            cs = slice(n * QBLK, (n + 1) * QBLK)
            l = ot[HEAD_DIM:HEAD_DIM + 1, cs]
            if has_sink:
                l = l + jnp.exp2(sink_ref[sink_row, h] * LOG2E - ms[n])
            outs.append(ot[0:HEAD_DIM, cs] * (1.0 / l))
            if emit_lse:
                lses.append(jnp.broadcast_to(ms[n] * LN2 + jnp.log(l), (HEAD_DIM, QBLK)))
        for n in range(0, len(gs), 2):
            c0 = (kvh * GQ + gs[n]) * HEAD_DIM
            o2 = jnp.concatenate(outs[n:n + 2], axis=0)
            o_ref[r0:r0 + QBLK, c0:c0 + LANES] = o2.T.astype(o_ref.dtype)
            if emit_lse:
                l2 = jnp.concatenate(lses[n:n + 2], axis=0)
                lse_ref[r0:r0 + QBLK, c0:c0 + LANES] = l2.T

    units = [(qb * QBLK, kvh) for qb in range(ts // QBLK) for kvh in range(n_kv)]
    head_sets = [tuple(range(i, i + HEADS_PER_PV)) for i in range(0, GQ, HEADS_PER_PV)]
    pending = [scores(*unit) for unit in units[:ATTN_LOOKAHEAD]]
    for u, unit in enumerate(units):
        if u + ATTN_LOOKAHEAD < len(units):
            pending.append(scores(*units[u + ATTN_LOOKAHEAD]))
        st = pending.pop(0)
        for gs in head_sets:
            pt, ms = softmax(st, *unit, gs)
            weighted_values(pt, ms, *unit, gs)


def _attention(q, kv, *, n_kv, halo, seq_len, dist_scale, slopes, sink=None,
               emit_lse=False, out_dtype=BF16):
    s, dq = q.shape
    dkv = kv.shape[1]
    ts = min(QBLK * (ATTN_UNITS_PER_STEP // n_kv), seq_len)
    assert seq_len % ts == 0 and ts % halo == 0 and dq == n_kv * GQ * HEAD_DIM
    assert dkv == 2 * n_kv * HEAD_DIM
    nh_blocks = s // halo
    per = ts // halo
    in_specs = [
        pl.BlockSpec((ts, dq), lambda i: (i, 0)),
        pl.BlockSpec((halo, dkv), lambda i: (jnp.maximum(i * per - 1, 0), 0)),
        pl.BlockSpec((ts, dkv), lambda i: (i, 0)),
        pl.BlockSpec((halo, dkv), lambda i: (jnp.minimum((i + 1) * per, nh_blocks - 1), 0)),
    ]
    args = [q, kv, kv, kv]
    if sink is not None:
        in_specs.append(pl.BlockSpec(memory_space=pltpu.SMEM))
        args.append(sink.arr)
    out_shape = [jax.ShapeDtypeStruct((s, dq), out_dtype)]
    out_specs = [pl.BlockSpec((ts, dq), lambda i: (i, 0))]
    if emit_lse:
        out_shape.append(jax.ShapeDtypeStruct((s, dq), F32))
        out_specs.append(pl.BlockSpec((ts, dq), lambda i: (i, 0)))
    kern = functools.partial(
        _attn_kernel, n_kv=n_kv, halo=halo, seq_len=seq_len, dist_scale=dist_scale,
        slopes=tuple(float(x) for x in slopes),
        sink_row=None if sink is None else sink.idx, emit_lse=emit_lse)
    return pl.pallas_call(
        kern,
        out_shape=tuple(out_shape),
        grid=(s // ts,),
        in_specs=in_specs,
        out_specs=tuple(out_specs),
        scratch_shapes=[
            pltpu.VMEM((ts + 2 * halo, dkv), BF16),
            pltpu.VMEM((n_kv * GQ + 1, QBLK + 2 * halo, QBLK), F32),
        ],
        compiler_params=_params(),
        name="banded_attn",
    )(*args)


def _silu(x):
    return x * jax.nn.sigmoid(x)


def _post_tail(get_o, x_ref, mod_ref, g_ref, wo_ref, wi_ref, wo2_ref, fin_ref, out_ref):
    tm = x_ref.shape[0]
    rows = tm // POST_ROW_SLICES
    dff = wo2_ref.shape[0]
    chunks = ((0, FFN_SPLIT), (FFN_SPLIT, dff))

    def gate_up(st, c):
        lo, hi = chunks[c]
        st["gu"] = (jnp.dot(st["h2"], wi_ref[:, lo:hi], preferred_element_type=F32),
                    jnp.dot(st["h2"], wi_ref[:, dff + lo:dff + hi], preferred_element_type=F32))

    def down(st, c):
        lo, hi = chunks[c]
        gate, up = st["gu_prev"]
        act = (_silu(gate) * up).astype(BF16)
        part = jnp.dot(act, wo2_ref[lo:hi, :], preferred_element_type=F32)
        st["acc"] = part if c == 0 else st["acc"] + part

    def stage(st, k):
        if k == 0:
            st["y"] = jnp.dot(get_o(st["rs"]), wo_ref[...], preferred_element_type=F32)
        elif k == 1:
            st["x1"] = x_ref[st["rs"], :] + mod_ref[2:3, :] * st["y"]
            st["h2"] = _norm_mod(st["x1"], g_ref[...], mod_ref[3:4, :],
                                 mod_ref[4:5, :]).astype(BF16)
            gate_up(st, 0)
        elif k <= len(chunks):
            st["gu_prev"] = st["gu"]
            gate_up(st, k - 1)
            down(st, k - 2)
        else:
            st["gu_prev"] = st["gu"]
            down(st, len(chunks) - 1)
            x2 = st["x1"] + mod_ref[5:6, :] * st["acc"]
            if fin_ref is not None:
                ms = jnp.mean(x2 * x2, axis=-1, keepdims=True)
                x2 = (x2 * lax.rsqrt(ms + RMS_EPS)) * fin_ref[...]
            out_ref[st["rs"], :] = x2

    states = [{"rs": slice(n * rows, (n + 1) * rows)} for n in range(POST_ROW_SLICES)]
    for k in range(len(chunks) + 2):
        for st in states:
            stage(st, k)


def _merged_o(attn_refs, slabs, om_ref):
    o0_ref, l0_ref, o1_ref, l1_ref, o2_ref, l2_ref = attn_refs
    so1, sl1, so2, sl2 = slabs
    nc = B_DQ // LANES

    def merged(rs):
        for src_o, src_l, dst_o, dst_l in ((o1_ref, l1_ref, so1, sl1), (o2_ref, l2_ref, so2, sl2)):
            dil = src_o.shape[0]
            n_src = (rs.stop - rs.start) // dil
            src_rs = slice(rs.start // dil, rs.start // dil + n_src)
            for rr in range(dil):
                dst_rows = pl.ds(rs.start + rr, n_src, stride=dil)
                for c in range(nc):
                    cs = slice(c * LANES, (c + 1) * LANES)
                    dst_o[c, dst_rows, :] = src_o[rr, src_rs, cs]
                    dst_l[c, dst_rows, :] = src_l[rr, src_rs, cs]
        for c in range(nc):
            cs = slice(c * LANES, (c + 1) * LANES)
            l0, l1, l2 = l0_ref[0, rs, cs], sl1[c, rs, :], sl2[c, rs, :]
            mx = jnp.maximum(jnp.maximum(l0, l1), l2)
            w0, w1, w2 = jnp.exp(l0 - mx), jnp.exp(l1 - mx), jnp.exp(l2 - mx)
            num = w0 * o0_ref[0, rs, cs] + w1 * so1[c, rs, :] + w2 * so2[c, rs, :]
            om_ref[rs, cs] = (num / (w0 + w1 + w2)).astype(BF16)
        return om_ref[rs, :]

    return merged


def _post_kernel(*refs, n_attn, final, cast_cols, has_ada):
    refs = list(refs)
    attn_refs = refs[:n_attn]
    x_ref, mod_ref, g_ref, wo_ref, wi_ref, wo2_ref = refs[n_attn:n_attn + 6]
    pos = n_attn + 6
    fin_ref = None
    if final:
        fin_ref = refs[pos]
        pos += 1
    n_cast = len(cast_cols)
    cast_src = refs[pos:pos + n_cast]
    pos += n_cast
    ada_in = refs[pos:pos + 3] if has_ada else None
    pos += 3 if has_ada else 0
    out_ref = refs[pos]
    cast_dst = refs[pos + 1:pos + 1 + n_cast]
    pos += 1 + n_cast
    ada_out = refs[pos] if has_ada else None
    pos += 1 if has_ada else 0
    scratch = refs[pos:]

    _run_casts(cast_src, cast_dst, cast_cols)
    if has_ada:
        _ada_block(*ada_in, ada_out)

    if n_attn == 1:
        (o_ref,) = attn_refs
        get_o = lambda rs: o_ref[rs, :]
    else:
        get_o = _merged_o(attn_refs, scratch[:4], scratch[4])
    _post_tail(get_o, x_ref, mod_ref, g_ref, wo_ref, wi_ref, wo2_ref, fin_ref, out_ref)


def _post(o_parts, x, mod, wo, g, wi, wo2, final_g, *, casts=(), next_ada=None):
    s, d = x.shape
    tm = 512
    steps = s // tm
    final = final_g is not None
    dilated = len(o_parts) > 1
    in_specs, args = [], []
    if dilated:
        for arr in o_parts:
            dil = arr.shape[0]
            in_specs.append(pl.BlockSpec((dil, tm // dil, arr.shape[2]), lambda i: (0, i, 0)))
            args.append(arr)
    else:
        (o,) = o_parts
        in_specs.append(pl.BlockSpec((tm, o.shape[1]), lambda i: (i, 0)))
        args.append(o)
    params = [mod, g, wo, wi, wo2] + ([final_g] if final else [])
    in_specs += [pl.BlockSpec((tm, d), lambda i: (i, 0))] + [_layer_spec(p) for p in params]
    args += [x] + [p.arr for p in params]
    out_shape = [jax.ShapeDtypeStruct((s, d), F32)]
    out_specs = [pl.BlockSpec((tm, d), lambda i: (i, 0))]
    cast_in, cast_args, cast_shape, cast_out = _cast_specs(casts, steps)
    in_specs += cast_in
    args += cast_args
    out_shape += cast_shape
    out_specs += cast_out
    if next_ada is not None:
        c_col, ada_w, ada_b3, layer = next_ada
        n = ada_w.shape[2]
        tn = 256
        last = n // tn - 1
        assert last < steps
        ada_in, ada_out = _ada_specs(ada_w, layer, tn, lambda i: jnp.minimum(i, last))
        in_specs += ada_in
        args += [c_col, ada_w, ada_b3]
        out_shape.append(jax.ShapeDtypeStruct((1, n), F32))
        out_specs.append(ada_out)
    scratch = []
    if dilated:
        scratch = [pltpu.VMEM((B_DQ // LANES, tm, LANES), F32) for _ in range(4)]
        scratch.append(pltpu.VMEM((tm, B_DQ), BF16))
    kern = functools.partial(
        _post_kernel, n_attn=len(o_parts), final=final,
        cast_cols=tuple(cst.cols for cst in casts), has_ada=next_ada is not None)
    return pl.pallas_call(
        kern,
        out_shape=tuple(out_shape),
        grid=(steps,),
        in_specs=in_specs,
        out_specs=tuple(out_specs),
        scratch_shapes=scratch,
        compiler_params=_params(),
        name="post_b" if dilated else "post_a",
    )(*args)


def _b_w_in_cols():
    ng = len(B_GROUPS)
    dq, dk = ng * B_DQ, ng * B_KV_HEADS * HEAD_DIM
    dkg = B_KV_HEADS * HEAD_DIM
    cols = []
    for gi in range(ng):
        cols += [(gi * B_DQ, B_DQ), (dq + gi * dkg, dkg), (dq + dk + gi * dkg, dkg)]
    return tuple(cols)


def kernel(x, c, ada_w, ada_b, norm_mix, norm_ffn, ffn_w_in, ffn_w_out,
           a_w_in, a_w_out, a_sink, b_w_in, b_w_out, final_norm):
    b, s, d = x.shape
    assert b == 1
    depth = ada_w.shape[0]
    xs = x.reshape(s, d)
    c_col = c.reshape(d, 1)
    ada_b3 = ada_b.reshape(depth, 1, -1)
    g_mix = norm_mix.reshape(depth, 1, d)
    g_ffn = norm_ffn.reshape(depth, 1, d)
    slopes_a = _alibi_slopes(A_Q_HEADS)
    slopes_b = _alibi_slopes(len(B_GROUPS) * B_Q_HEADS)
    b_cols = _b_w_in_cols()

    def mixer_params(i):
        return (a_w_in, a_w_out, i // 2, None) if i % 2 == 0 else (b_w_in, b_w_out, i // 2, b_cols)

    mod = _ada_mod(c_col, ada_w, ada_b3, 0)
    w_in = _Layer(a_w_in[0].astype(BF16)[None], 0)
    casts0 = (_Cast(a_w_out, 0, 64), _Cast(ffn_w_in, 0, 64), _Cast(ffn_w_out, 0, 176))
    for i in range(depth):
        mod_l = _Layer(mod.reshape(1, 6, d), 0)
        if i % 2 == 0:
            q, kv, *cast0 = _qkv_a(xs, mod_l, _Layer(g_mix, i), w_in, casts0 if i == 0 else ())
            if i == 0:
                w_out, wi, wo2 = (_Layer(r[None], 0) for r in cast0)
            o_parts = _attention(q, kv, n_kv=A_KV_HEADS, halo=A_HALF_WINDOW, seq_len=s,
                                 dist_scale=1, slopes=slopes_a, sink=_Layer(a_sink, i // 2))
        else:
            outs = _qkv_b(xs, mod_l, _Layer(g_mix, i), w_in)
            o_parts = []
            for gi, (window, dil) in enumerate(B_GROUPS):
                qg = outs[2 * gi].reshape(s, B_DQ)
                kvg = outs[2 * gi + 1].reshape(s, B_DKV)
                og, lg = _attention(
                    qg, kvg, n_kv=B_KV_HEADS, halo=window // (2 * dil), seq_len=s // dil,
                    dist_scale=dil, slopes=slopes_b[gi * B_Q_HEADS:(gi + 1) * B_Q_HEADS],
                    emit_lse=True, out_dtype=F32)
                o_parts += [og.reshape(dil, s // dil, B_DQ), lg.reshape(dil, s // dil, B_DQ)]
        last = i == depth - 1
        casts, next_ada = (), None
        if not last:
            m_in, m_out, j, cols = mixer_params(i + 1)
            casts = (_Cast(m_in, j, 32, cols), _Cast(m_out, j, m_out.shape[1] // 32),
                     _Cast(ffn_w_in, i + 1, 32), _Cast(ffn_w_out, i + 1, 176))
            next_ada = (c_col, ada_w, ada_b3, i + 1)
        fin = _Layer(final_norm.reshape(1, 1, d), 0) if last else None
        res = _post(o_parts, xs, mod_l, w_out, _Layer(g_ffn, i), wi, wo2, fin,
                    casts=casts, next_ada=next_ada)
        xs = res[0]
        if not last:
            w_in, w_out, wi, wo2 = (_Layer(r[None], 0) for r in res[1:5])
            mod = res[5]
    return xs.reshape(b, s, d)
```

```python
import functools
from typing import NamedTuple

import numpy as np
import jax
import jax.numpy as jnp
from jax import lax
from jax.experimental import pallas as pl
from jax.experimental.pallas import tpu as pltpu

HEAD_DIM = 64
A_Q_HEADS = 16
A_KV_HEADS = 4
A_HALF_WINDOW = 128
B_GROUPS = ((128, 1), (512, 4), (2048, 16))
B_Q_HEADS = 8
B_KV_HEADS = 2
GQ = 4
RMS_EPS = 1e-6
NEG = -1e30
LOG2E = 1.4426950408889634
LN2 = 0.6931471805599453
Q_SCALE = HEAD_DIM ** -0.5 * LOG2E

LANES = 128
QBLK = 128
ATTN_UNITS_PER_STEP = 32
ONES_COLS = 32
HEADS_PER_PV = 4
A_PV_DELAY, B_PV_DELAY = 0, 1
FFN_SPLIT = 1536
QKV_ROW_SLICES = 4
POST_ROW_SLICES = 2
VMEM_LIMIT = 56 * 1024 * 1024

BF16 = jnp.bfloat16
F32 = jnp.float32


def _alibi_slopes(n):
    return np.asarray(2.0 ** (-8.0 * np.arange(1, n + 1) / n), dtype=np.float32)


class _Layer(NamedTuple):
    arr: jax.Array
    idx: int


def _layer_spec(p):
    nd = p.arr.ndim
    return pl.BlockSpec((None,) + p.arr.shape[1:], lambda *_: (p.idx,) + (0,) * (nd - 1),
                        pipeline_mode=pl.Buffered(1))


def _params(n_axes=1, **kwargs):
    return pltpu.CompilerParams(
        dimension_semantics=("arbitrary",) * n_axes, vmem_limit_bytes=VMEM_LIMIT, **kwargs)


def _ada_block(c_ref, w_ref, b_ref, o_ref):
    c = c_ref[...]
    cond = c * jax.nn.sigmoid(c)
    o_ref[...] = jnp.sum(w_ref[...] * cond, axis=0, keepdims=True) + b_ref[...]


def _ada_specs(ada_w, layer, tn, col_block):
    d = ada_w.shape[1]
    return ([pl.BlockSpec((d, 1), lambda *i: (0, 0)),
             pl.BlockSpec((None, d, tn), lambda *i: (layer, 0, col_block(*i))),
             pl.BlockSpec((None, 1, tn), lambda *i: (layer, 0, col_block(*i)))],
            pl.BlockSpec((1, tn), lambda *i: (0, col_block(*i))))


def _ada_mod(c_col, ada_w, ada_b3, layer):
    n = ada_w.shape[2]
    tn = 1536
    in_specs, out_spec = _ada_specs(ada_w, layer, tn, lambda j: j)
    return pl.pallas_call(
        _ada_block,
        out_shape=jax.ShapeDtypeStruct((1, n), F32),
        grid=(n // tn,),
        in_specs=in_specs,
        out_specs=out_spec,
        compiler_params=_params(),
        name="ada_mod",
    )(c_col, ada_w, ada_b3)


def _norm_mod(x, g, shift, scale):
    ms = jnp.mean(x * x, axis=-1, keepdims=True)
    y = x * lax.rsqrt(ms + RMS_EPS)
    return (y * g) * (1.0 + scale) + shift


class _Cast(NamedTuple):
    arr: jax.Array
    idx: int
    rows: int
    cols: tuple | None = None


def _cast_specs(casts, steps):
    in_specs, args, out_shape, out_specs = [], [], [], []
    for cst in casts:
        _, n_rows, n_cols = cst.arr.shape
        last = n_rows // cst.rows - 1
        assert n_rows % cst.rows == 0 and last < steps
        in_specs.append(pl.BlockSpec((None, cst.rows, n_cols),
                                     lambda i, cst=cst, last=last: (cst.idx, jnp.minimum(i, last), 0)))
        args.append(cst.arr)
        out_shape.append(jax.ShapeDtypeStruct((n_rows, n_cols), BF16))
        out_specs.append(pl.BlockSpec((cst.rows, n_cols),
                                      lambda i, last=last: (jnp.minimum(i, last), 0)))
    return in_specs, args, out_shape, out_specs


def _run_casts(srcs, dsts, cast_cols):
    for src, dst, cols in zip(srcs, dsts, cast_cols):
        if cols is None:
            dst[...] = src[...].astype(BF16)
        else:
            at = 0
            for start, width in cols:
                dst[:, at:at + width] = src[:, start:start + width].astype(BF16)
                at += width


def _qkv_a_kernel(x_ref, mod_ref, g_ref, w_ref, *refs, cast_cols):
    n_cast = len(cast_cols)
    cast_src, (q_ref, kv_ref), cast_dst = refs[:n_cast], refs[n_cast:n_cast + 2], refs[n_cast + 2:]
    _run_casts(cast_src, cast_dst, cast_cols)
    dq = q_ref.shape[1]
    rows = x_ref.shape[0] // QKV_ROW_SLICES
    for n in range(QKV_ROW_SLICES):
        rs = slice(n * rows, (n + 1) * rows)
        h = _norm_mod(x_ref[rs, :], g_ref[...], mod_ref[0:1, :], mod_ref[1:2, :])
        r = jnp.dot(h.astype(BF16), w_ref[...], preferred_element_type=F32)
        q_ref[rs, :] = (r[:, :dq] * Q_SCALE).astype(BF16)
        kv_ref[rs, :] = r[:, dq:].astype(BF16)


def _qkv_a(x, mod, g, w, casts=()):
    s, d = x.shape
    dq = A_Q_HEADS * HEAD_DIM
    dkv = w.arr.shape[2] - dq
    tm = 1024
    cast_in, cast_args, cast_shape, cast_out = _cast_specs(casts, s // tm)
    return pl.pallas_call(
        functools.partial(_qkv_a_kernel, cast_cols=tuple(cst.cols for cst in casts)),
        out_shape=(jax.ShapeDtypeStruct((s, dq), BF16),
                   jax.ShapeDtypeStruct((s, dkv), BF16), *cast_shape),
        grid=(s // tm,),
        in_specs=[
            pl.BlockSpec((tm, d), lambda i: (i, 0)),
            _layer_spec(mod),
            _layer_spec(g),
            _layer_spec(w),
            *cast_in,
        ],
        out_specs=(pl.BlockSpec((tm, dq), lambda i: (i, 0)),
                   pl.BlockSpec((tm, dkv), lambda i: (i, 0)), *cast_out),
        compiler_params=_params(),
        name="qkv_a",
    )(x, mod.arr, g.arr, w.arr, *cast_args)


B_DQ = B_Q_HEADS * HEAD_DIM
B_DKV = 2 * B_KV_HEADS * HEAD_DIM
B_DG = B_DQ + B_DKV


def _qkv_b_kernel(x_ref, mod_ref, g_ref, w_ref, *refs):
    n_out = 2 * len(B_GROUPS)
    out_refs, slabs = refs[:n_out], refs[n_out:]
    tm = x_ref.shape[0]
    rows = tm // QKV_ROW_SLICES
    nq = B_DQ // LANES
    for n in range(QKV_ROW_SLICES):
        rs = slice(n * rows, (n + 1) * rows)
        h = _norm_mod(x_ref[rs, :], g_ref[...], mod_ref[0:1, :], mod_ref[1:2, :])
        hb = h.astype(BF16)
        for gi, (_, dil) in enumerate(B_GROUPS):
            r = jnp.dot(hb, w_ref[:, gi * B_DG:(gi + 1) * B_DG], preferred_element_type=F32)
            rq = r[:, :B_DQ] * Q_SCALE
            rkv = r[:, B_DQ:]
            if dil == 1:
                out_refs[0][0, rs, :] = rq.astype(BF16)
                out_refs[1][0, rs, :] = rkv.astype(BF16)
                continue
            for c in range(B_DG // LANES):
                src = rq if c < nq else rkv
                cc = c if c < nq else c - nq
                slabs[gi - 1][c, rs, :] = src[:, cc * LANES:(cc + 1) * LANES]
    for gi, (_, dil) in enumerate(B_GROUPS):
        if dil == 1:
            continue
        q_ref, kv_ref = out_refs[2 * gi], out_refs[2 * gi + 1]
        for rr in range(dil):
            for c in range(B_DG // LANES):
                v = slabs[gi - 1][c, pl.ds(rr, tm // dil, stride=dil), :].astype(BF16)
                if c < nq:
                    q_ref[rr, :, c * LANES:(c + 1) * LANES] = v
                else:
                    kv_ref[rr, :, (c - nq) * LANES:(c - nq + 1) * LANES] = v


def _qkv_b(x, mod, g, w):
    s, d = x.shape
    tm = 1024
    out_shape, out_specs = [], []
    for _, dil in B_GROUPS:
        for width in (B_DQ, B_DKV):
            out_shape.append(jax.ShapeDtypeStruct((dil, s // dil, width), BF16))
            out_specs.append(pl.BlockSpec((dil, tm // dil, width), lambda i: (0, i, 0)))
    return pl.pallas_call(
        _qkv_b_kernel,
        out_shape=tuple(out_shape),
        grid=(s // tm,),
        in_specs=[
            pl.BlockSpec((tm, d), lambda i: (i, 0)),
            _layer_spec(mod),
            _layer_spec(g),
            _layer_spec(w),
        ],
        out_specs=tuple(out_specs),
        scratch_shapes=[pltpu.VMEM((B_DG // LANES, tm, LANES), F32)
                        for _, dil in B_GROUPS if dil > 1],
        compiler_params=_params(),
        name="qkv_b",
    )(x, mod.arr, g.arr, w.arr)


def _attn_units(q_ref, kv_refs, sink_ref, o_ref, lse_ref, kvbuf, bias_ref, tile, first_step,
                *, n_kv, halo, seq_len, dist_scale, slopes, sink_row):
    kvp_ref, kvc_ref, kvn_ref = kv_refs
    has_sink = sink_row is not None
    emit_lse = lse_ref is not None
    ts = q_ref.shape[0]
    w = QBLK + 2 * halo
    nh = n_kv * GQ

    @pl.when(first_step)
    def _():
        kj = lax.broadcasted_iota(jnp.int32, (w, QBLK), 0)
        qi = lax.broadcasted_iota(jnp.int32, (w, QBLK), 1)
        rel = jnp.abs(kj - halo - qi)
        dist = (dist_scale * rel).astype(F32)
        for h, slope in enumerate(slopes):
            bias_ref[h] = jnp.where(rel <= halo, ((-float(slope)) * dist) * LOG2E, NEG)
        bias_ref[nh] = jnp.full((w, QBLK), NEG, F32)

    kvbuf, vt_ref = kvbuf
    kvbuf[0:halo, :] = kvp_ref[...]
    kvbuf[halo:halo + ts, :] = kvc_ref[...]
    kvbuf[halo + ts:, :] = kvn_ref[...]
    dkv = n_kv * HEAD_DIM
    for pair in range(n_kv // 2):
        vt_ref[pair * LANES:(pair + 1) * LANES, :] = (
            kvbuf[:, dkv + pair * LANES:dkv + (pair + 1) * LANES].T)

    ones = jnp.ones((ONES_COLS, w), BF16)
    nt_dims = (((1,), (1,)), ((), ()))

    def scores(r0, kvh):
        c0 = kvh * GQ * HEAD_DIM
        qg = q_ref[pl.ds(r0, QBLK), c0:c0 + GQ * HEAD_DIM]
        qs = jnp.concatenate(
            [qg[:, g * HEAD_DIM:(g + 1) * HEAD_DIM] for g in range(GQ)], axis=0)
        k = kvbuf[pl.ds(r0, w), kvh * HEAD_DIM:(kvh + 1) * HEAD_DIM]
        return lax.dot_general(k, qs, nt_dims, preferred_element_type=F32)

    def softmax(st, r0, kvh, gs):
        seq_pos = (tile * ts + r0) & (seq_len - 1)
        prev_out = seq_pos == 0
        next_out = seq_pos + QBLK == seq_len
        ps, ms = [], []
        for g in gs:
            h = kvh * GQ + g
            top = jnp.where(prev_out, nh, h)
            bot = jnp.where(next_out, nh, h)
            bias = jnp.concatenate([bias_ref[top, 0:halo, :],
                                    bias_ref[h, halo:halo + QBLK, :],
                                    bias_ref[bot, halo + QBLK:w, :]], axis=0)
            sg = st[:, g * QBLK:(g + 1) * QBLK] + bias
            m = jnp.max(sg, axis=0, keepdims=True)
            if has_sink:
                m = jnp.maximum(m, sink_ref[sink_row, h] * LOG2E)
            ps.append(jnp.exp2(sg - m).astype(BF16))
            ms.append(m)
        return jnp.concatenate(ps, axis=1), ms

    def weighted_values(pt, ms, r0, kvh, gs):
        vt = vt_ref[kvh * HEAD_DIM:(kvh + 1) * HEAD_DIM, r0:r0 + w]
        vt_ext = jnp.concatenate([vt, ones], axis=0)
        ot = jnp.dot(vt_ext, pt, preferred_element_type=F32)
        outs, lses = [], []
        for n, g in enumerate(gs):
            h = kvh * GQ + g
            cs = slice(n * QBLK, (n + 1) * QBLK)
            l = ot[HEAD_DIM:HEAD_DIM + 1, cs]
            if has_sink:
                l = l + jnp.exp2(sink_ref[sink_row, h] * LOG2E - ms[n])
            outs.append(ot[0:HEAD_DIM, cs] * (1.0 / l))
            if emit_lse:
                lses.append(jnp.broadcast_to(ms[n] * LN2 + jnp.log(l), (HEAD_DIM, QBLK)))
        for n in range(0, len(gs), 2):
            c0 = (kvh * GQ + gs[n]) * HEAD_DIM
            o2 = jnp.concatenate(outs[n:n + 2], axis=0)
            o_ref[r0:r0 + QBLK, c0:c0 + LANES] = o2.T.astype(o_ref.dtype)
            if emit_lse:
                l2 = jnp.concatenate(lses[n:n + 2], axis=0)
                lse_ref[r0:r0 + QBLK, c0:c0 + LANES] = l2.T

    units = [(qb * QBLK, kvh) for qb in range(ts // QBLK) for kvh in range(n_kv)]
    head_sets = [tuple(range(i, i + HEADS_PER_PV)) for i in range(0, GQ, HEADS_PER_PV)]
    pending, probs = {}, {}

    def issue(u):
        pending[u] = scores(*units[u])

    def soft(u):
        st = pending.pop(u)
        probs[u] = [softmax(st, *units[u], gs) for gs in head_sets]

    def finish(u):
        for gs, (pt, ms) in zip(head_sets, probs.pop(u)):
            weighted_values(pt, ms, *units[u], gs)

    return [(functools.partial(issue, u), functools.partial(soft, u), functools.partial(finish, u))
            for u in range(len(units))]


def _attn_kernel(*refs, emit_lse, pv_delay, **cfg):
    refs = list(refs)
    q_ref, kv_refs = refs[0], refs[1:4]
    pos = 4
    sink_ref = None
    if cfg["sink_row"] is not None:
        sink_ref = refs[pos]
        pos += 1
    o_ref = refs[pos]
    pos += 1
    lse_ref = None
    if emit_lse:
        lse_ref = refs[pos]
        pos += 1
    kvbufs, bias_ref = refs[pos:pos + 2], refs[pos + 2]
    step = pl.program_id(0)
    units = _attn_units(q_ref, kv_refs, sink_ref, o_ref, lse_ref, kvbufs, bias_ref,
                        step, step == 0, **cfg)
    units[0][0]()
    for u, (_, soft, _) in enumerate(units):
        if u + 1 < len(units):
            units[u + 1][0]()
        soft()
        if u >= pv_delay:
            units[u - pv_delay][2]()
    for u in range(max(len(units) - pv_delay, 0), len(units)):
        units[u][2]()


def _attn_in_specs(s, ts, dq, dkv, halo):
    per = ts // halo
    last_halo = s // halo - 1
    return [
        pl.BlockSpec((ts, dq), lambda i: (i, 0)),
        pl.BlockSpec((halo, dkv), lambda i: (jnp.maximum(i * per - 1, 0), 0)),
        pl.BlockSpec((ts, dkv), lambda i: (i, 0)),
        pl.BlockSpec((halo, dkv), lambda i: (jnp.minimum((i + 1) * per, last_halo), 0)),
    ]


def _attn_scratch(ts, dkv, n_kv, halo):
    return [pltpu.VMEM((ts + 2 * halo, dkv), BF16),
            pltpu.VMEM((n_kv * HEAD_DIM, ts + 2 * halo), BF16),
            pltpu.VMEM((n_kv * GQ + 1, QBLK + 2 * halo, QBLK), F32)]


def _attention(q, kv, *, n_kv, halo, seq_len, dist_scale, slopes, pv_delay, sink=None,
               emit_lse=False, out_dtype=BF16):
    s, dq = q.shape
    dkv = kv.shape[1]
    ts = min(QBLK * (ATTN_UNITS_PER_STEP // n_kv), seq_len)
    assert seq_len % ts == 0 and ts % halo == 0 and dq == n_kv * GQ * HEAD_DIM
    assert dkv == 2 * n_kv * HEAD_DIM
    in_specs = _attn_in_specs(s, ts, dq, dkv, halo)
    args = [q, kv, kv, kv]
    if sink is not None:
        in_specs.append(pl.BlockSpec(memory_space=pltpu.SMEM))
        args.append(sink.arr)
    out_shape = [jax.ShapeDtypeStruct((s, dq), out_dtype)]
    out_specs = [pl.BlockSpec((ts, dq), lambda i: (i, 0))]
    if emit_lse:
        out_shape.append(jax.ShapeDtypeStruct((s, dq), F32))
        out_specs.append(pl.BlockSpec((ts, dq), lambda i: (i, 0)))
    kern = functools.partial(
        _attn_kernel, emit_lse=emit_lse, pv_delay=pv_delay, n_kv=n_kv, halo=halo, seq_len=seq_len,
        dist_scale=dist_scale, slopes=tuple(float(x) for x in slopes),
        sink_row=None if sink is None else sink.idx)
    return pl.pallas_call(
        kern,
        out_shape=tuple(out_shape),
        grid=(s // ts,),
        in_specs=in_specs,
        out_specs=tuple(out_specs),
        scratch_shapes=_attn_scratch(ts, dkv, n_kv, halo),
        compiler_params=_params(),
        name="banded_attn",
    )(*args)


def _silu(x):
    return x * jax.nn.sigmoid(x)


def _post_tail(get_o, x_ref, mod_ref, g_ref, wo_ref, wi_ref, wo2_ref, fin_ref, out_ref):
    tm = x_ref.shape[0]
    rows = tm // POST_ROW_SLICES
    dff = wo2_ref.shape[0]
    chunks = ((0, FFN_SPLIT), (FFN_SPLIT, dff))

    def gate_up(st, c):
        lo, hi = chunks[c]
        st["gu"] = (jnp.dot(st["h2"], wi_ref[:, lo:hi], preferred_element_type=F32),
                    jnp.dot(st["h2"], wi_ref[:, dff + lo:dff + hi], preferred_element_type=F32))

    def down(st, c):
        lo, hi = chunks[c]
        gate, up = st["gu_prev"]
        act = (_silu(gate) * up).astype(BF16)
        part = jnp.dot(act, wo2_ref[lo:hi, :], preferred_element_type=F32)
        st["acc"] = part if c == 0 else st["acc"] + part

    def stage(st, k):
        if k == 0:
            st["y"] = jnp.dot(get_o(st["rs"]), wo_ref[...], preferred_element_type=F32)
        elif k == 1:
            st["x1"] = x_ref[st["rs"], :] + mod_ref[2:3, :] * st["y"]
            st["h2"] = _norm_mod(st["x1"], g_ref[...], mod_ref[3:4, :],
                                 mod_ref[4:5, :]).astype(BF16)
            gate_up(st, 0)
        elif k <= len(chunks):
            st["gu_prev"] = st["gu"]
            gate_up(st, k - 1)
            down(st, k - 2)
        else:
            st["gu_prev"] = st["gu"]
            down(st, len(chunks) - 1)
            x2 = st["x1"] + mod_ref[5:6, :] * st["acc"]
            if fin_ref is not None:
                ms = jnp.mean(x2 * x2, axis=-1, keepdims=True)
                x2 = (x2 * lax.rsqrt(ms + RMS_EPS)) * fin_ref[...]
            out_ref[st["rs"], :] = x2

    states = [{"rs": slice(n * rows, (n + 1) * rows)} for n in range(POST_ROW_SLICES)]
    for k in range(len(chunks) + 2):
        for st in states:
            stage(st, k)


def _merged_o(attn_refs, slabs, om_ref):
    o0_ref, l0_ref, o1_ref, l1_ref, o2_ref, l2_ref = attn_refs
    so1, sl1, so2, sl2 = slabs
    nc = B_DQ // LANES

    def merged(rs):
        for src_o, src_l, dst_o, dst_l in ((o1_ref, l1_ref, so1, sl1), (o2_ref, l2_ref, so2, sl2)):
            dil = src_o.shape[0]
            n_src = (rs.stop - rs.start) // dil
            src_rs = slice(rs.start // dil, rs.start // dil + n_src)
            for rr in range(dil):
                dst_rows = pl.ds(rs.start + rr, n_src, stride=dil)
                for c in range(nc):
                    cs = slice(c * LANES, (c + 1) * LANES)
                    dst_o[c, dst_rows, :] = src_o[rr, src_rs, cs]
                    dst_l[c, dst_rows, :] = src_l[rr, src_rs, cs]
        for c in range(nc):
            cs = slice(c * LANES, (c + 1) * LANES)
            l0, l1, l2 = l0_ref[0, rs, cs], sl1[c, rs, :], sl2[c, rs, :]
            mx = jnp.maximum(jnp.maximum(l0, l1), l2)
            w0, w1, w2 = jnp.exp(l0 - mx), jnp.exp(l1 - mx), jnp.exp(l2 - mx)
            num = w0 * o0_ref[0, rs, cs] + w1 * so1[c, rs, :] + w2 * so2[c, rs, :]
            om_ref[rs, cs] = (num / (w0 + w1 + w2)).astype(BF16)
        return om_ref[rs, :]

    return merged


def _post_kernel(*refs, n_attn, final, cast_cols, has_ada):
    refs = list(refs)
    attn_refs = refs[:n_attn]
    x_ref, mod_ref, g_ref, wo_ref, wi_ref, wo2_ref = refs[n_attn:n_attn + 6]
    pos = n_attn + 6
    fin_ref = None
    if final:
        fin_ref = refs[pos]
        pos += 1
    n_cast = len(cast_cols)
    cast_src = refs[pos:pos + n_cast]
    pos += n_cast
    ada_in = refs[pos:pos + 3] if has_ada else None
    pos += 3 if has_ada else 0
    out_ref = refs[pos]
    cast_dst = refs[pos + 1:pos + 1 + n_cast]
    pos += 1 + n_cast
    ada_out = refs[pos] if has_ada else None
    pos += 1 if has_ada else 0
    scratch = refs[pos:]

    _run_casts(cast_src, cast_dst, cast_cols)
    if has_ada:
        _ada_block(*ada_in, ada_out)

    if n_attn == 1:
        (o_ref,) = attn_refs
        get_o = lambda rs: o_ref[rs, :]
    else:
        get_o = _merged_o(attn_refs, scratch[:4], scratch[4])
    _post_tail(get_o, x_ref, mod_ref, g_ref, wo_ref, wi_ref, wo2_ref, fin_ref, out_ref)


def _post(o_parts, x, mod, wo, g, wi, wo2, final_g, *, casts=(), next_ada=None):
    s, d = x.shape
    tm = 512
    steps = s // tm
    final = final_g is not None
    dilated = len(o_parts) > 1
    in_specs, args, scratch = [], [], []
    if dilated:
        for arr in o_parts:
            dil = arr.shape[0]
            in_specs.append(pl.BlockSpec((dil, tm // dil, arr.shape[2]), lambda i: (0, i, 0)))
            args.append(arr)
        scratch = [pltpu.VMEM((B_DQ // LANES, tm, LANES), F32) for _ in range(4)]
        scratch.append(pltpu.VMEM((tm, B_DQ), BF16))
    else:
        (o,) = o_parts
        in_specs.append(pl.BlockSpec((tm, o.shape[1]), lambda i: (i, 0)))
        args.append(o)
    params = [mod, g, wo, wi, wo2] + ([final_g] if final else [])
    in_specs += [pl.BlockSpec((tm, d), lambda i: (i, 0))] + [_layer_spec(p) for p in params]
    args += [x] + [p.arr for p in params]
    out_shape = [jax.ShapeDtypeStruct((s, d), F32)]
    out_specs = [pl.BlockSpec((tm, d), lambda i: (i, 0))]
    cast_in, cast_args, cast_shape, cast_out = _cast_specs(casts, steps)
    in_specs += cast_in
    args += cast_args
    out_shape += cast_shape
    out_specs += cast_out
    if next_ada is not None:
        c_col, ada_w, ada_b3, layer = next_ada
        n = ada_w.shape[2]
        tn = 256
        last = n // tn - 1
        assert last < steps
        ada_in, ada_out = _ada_specs(ada_w, layer, tn, lambda i: jnp.minimum(i, last))
        in_specs += ada_in
        args += [c_col, ada_w, ada_b3]
        out_shape.append(jax.ShapeDtypeStruct((1, n), F32))
        out_specs.append(ada_out)
    kern = functools.partial(
        _post_kernel, n_attn=len(o_parts), final=final,
        cast_cols=tuple(cst.cols for cst in casts), has_ada=next_ada is not None)
    return pl.pallas_call(
        kern,
        out_shape=tuple(out_shape),
        grid=(steps,),
        in_specs=in_specs,
        out_specs=tuple(out_specs),
        scratch_shapes=scratch,
        compiler_params=_params(),
        name="post_b" if dilated else "post_a",
    )(*args)


def _b_w_in_cols():
    ng = len(B_GROUPS)
    dq, dk = ng * B_DQ, ng * B_KV_HEADS * HEAD_DIM
    dkg = B_KV_HEADS * HEAD_DIM
    cols = []
    for gi in range(ng):
        cols += [(gi * B_DQ, B_DQ), (dq + gi * dkg, dkg), (dq + dk + gi * dkg, dkg)]
    return tuple(cols)


def kernel(x, c, ada_w, ada_b, norm_mix, norm_ffn, ffn_w_in, ffn_w_out,
           a_w_in, a_w_out, a_sink, b_w_in, b_w_out, final_norm):
    b, s, d = x.shape
    assert b == 1
    depth = ada_w.shape[0]
    xs = x.reshape(s, d)
    c_col = c.reshape(d, 1)
    ada_b3 = ada_b.reshape(depth, 1, -1)
    g_mix = norm_mix.reshape(depth, 1, d)
    g_ffn = norm_ffn.reshape(depth, 1, d)
    slopes_a = _alibi_slopes(A_Q_HEADS)
    slopes_b = _alibi_slopes(len(B_GROUPS) * B_Q_HEADS)
    b_cols = _b_w_in_cols()

    def mixer_params(i):
        return (a_w_in, a_w_out, i // 2, None) if i % 2 == 0 else (b_w_in, b_w_out, i // 2, b_cols)

    mod = _ada_mod(c_col, ada_w, ada_b3, 0)
    w_in = _Layer(a_w_in[0].astype(BF16)[None], 0)
    casts0 = (_Cast(a_w_out, 0, 64), _Cast(ffn_w_in, 0, 64), _Cast(ffn_w_out, 0, 176))
    for i in range(depth):
        mod_l = _Layer(mod.reshape(1, 6, d), 0)
        if i % 2 == 0:
            q, kv, *cast0 = _qkv_a(xs, mod_l, _Layer(g_mix, i), w_in, casts0 if i == 0 else ())
            if i == 0:
                w_out, wi, wo2 = (_Layer(r[None], 0) for r in cast0)
            o_parts = _attention(q, kv, n_kv=A_KV_HEADS, halo=A_HALF_WINDOW, seq_len=s,
                                 dist_scale=1, slopes=slopes_a, pv_delay=A_PV_DELAY,
                                 sink=_Layer(a_sink, i // 2))
        else:
            outs = _qkv_b(xs, mod_l, _Layer(g_mix, i), w_in)
            o_parts = []
            for gi, (window, dil) in enumerate(B_GROUPS):
                qg = outs[2 * gi].reshape(s, B_DQ)
                kvg = outs[2 * gi + 1].reshape(s, B_DKV)
                og, lg = _attention(
                    qg, kvg, n_kv=B_KV_HEADS, halo=window // (2 * dil), seq_len=s // dil,
                    dist_scale=dil, slopes=slopes_b[gi * B_Q_HEADS:(gi + 1) * B_Q_HEADS],
                    pv_delay=B_PV_DELAY, emit_lse=True, out_dtype=F32)
                o_parts += [og.reshape(dil, s // dil, B_DQ), lg.reshape(dil, s // dil, B_DQ)]
        last = i == depth - 1
        casts, next_ada = (), None
        if not last:
            m_in, m_out, j, cols = mixer_params(i + 1)
            casts = (_Cast(m_in, j, 32, cols), _Cast(m_out, j, m_out.shape[1] // 32),
                     _Cast(ffn_w_in, i + 1, 32), _Cast(ffn_w_out, i + 1, 176))
            next_ada = (c_col, ada_w, ada_b3, i + 1)
        fin = _Layer(final_norm.reshape(1, 1, d), 0) if last else None
        res = _post(o_parts, xs, mod_l, w_out, _Layer(g_ffn, i), wi, wo2, fin,
                    casts=casts, next_ada=next_ada)
        xs = res[0]
        if not last:
            w_in, w_out, wi, wo2 = (_Layer(r[None], 0) for r in res[1:5])
            mod = res[5]
    return xs.reshape(b, s, d)
```

```python
import functools
from typing import NamedTuple

import numpy as np
import jax
import jax.numpy as jnp
from jax import lax
from jax.experimental import pallas as pl
from jax.experimental.pallas import tpu as pltpu

HEAD_DIM = 64
A_Q_HEADS = 16
A_KV_HEADS = 4
A_HALF_WINDOW = 128
B_GROUPS = ((128, 1), (512, 4), (2048, 16))
B_Q_HEADS = 8
B_KV_HEADS = 2
GQ = 4
RMS_EPS = 1e-6
NEG = -1e30
LOG2E = 1.4426950408889634
LN2 = 0.6931471805599453
Q_SCALE = HEAD_DIM ** -0.5 * LOG2E

LANES = 128
QBLK = 128
A_UNITS_PER_STEP, B_UNITS_PER_STEP = 32, 64
ONES_COLS = 32
HEADS_PER_PV = 4
A_PV_DELAY, B_PV_DELAY = 0, 1
FFN_SPLIT = 1536
QKV_ROW_SLICES = 2
POST_ROW_SLICES = 2
VMEM_LIMIT = 56 * 1024 * 1024

BF16 = jnp.bfloat16
F32 = jnp.float32


def _alibi_slopes(n):
    return np.asarray(2.0 ** (-8.0 * np.arange(1, n + 1) / n), dtype=np.float32)


class _Layer(NamedTuple):
    arr: jax.Array
    idx: int


def _layer_spec(p):
    nd = p.arr.ndim
    return pl.BlockSpec((None,) + p.arr.shape[1:], lambda *_: (p.idx,) + (0,) * (nd - 1),
                        pipeline_mode=pl.Buffered(1))


def _params(n_axes=1, **kwargs):
    return pltpu.CompilerParams(
        dimension_semantics=("arbitrary",) * n_axes, vmem_limit_bytes=VMEM_LIMIT, **kwargs)


def _ada_block(c_ref, w_ref, b_ref, o_ref):
    c = c_ref[...]
    cond = c * jax.nn.sigmoid(c)
    o_ref[...] = jnp.sum(w_ref[...] * cond, axis=0, keepdims=True) + b_ref[...]


def _ada_specs(ada_w, layer, tn, col_block):
    d = ada_w.shape[1]
    return ([pl.BlockSpec((d, 1), lambda *i: (0, 0)),
             pl.BlockSpec((None, d, tn), lambda *i: (layer, 0, col_block(*i))),
             pl.BlockSpec((None, 1, tn), lambda *i: (layer, 0, col_block(*i)))],
            pl.BlockSpec((1, tn), lambda *i: (0, col_block(*i))))


def _ada_mod(c_col, ada_w, ada_b3, layer):
    n = ada_w.shape[2]
    tn = 1536
    in_specs, out_spec = _ada_specs(ada_w, layer, tn, lambda j: j)
    return pl.pallas_call(
        _ada_block,
        out_shape=jax.ShapeDtypeStruct((1, n), F32),
        grid=(n // tn,),
        in_specs=in_specs,
        out_specs=out_spec,
        compiler_params=_params(),
        name="ada_mod",
    )(c_col, ada_w, ada_b3)


def _norm_mod(x, g, shift, scale):
    ms = jnp.mean(x * x, axis=-1, keepdims=True)
    y = x * lax.rsqrt(ms + RMS_EPS)
    return (y * g) * (1.0 + scale) + shift


class _Cast(NamedTuple):
    arr: jax.Array
    idx: int
    cols: tuple | None = None


BF16_SUBLANES = 16


def _cast_specs(casts, steps):
    in_specs, args, out_shape, out_specs = [], [], [], []
    for cst in casts:
        _, n_rows, n_cols = cst.arr.shape
        n_blk = next(n for n in range(steps, 0, -1)
                     if n_rows % n == 0 and (n_rows // n) % BF16_SUBLANES == 0)
        rows, last = n_rows // n_blk, n_blk - 1
        in_specs.append(pl.BlockSpec((None, rows, n_cols),
                                     lambda i, cst=cst, last=last: (cst.idx, jnp.minimum(i, last), 0)))
        args.append(cst.arr)
        out_shape.append(jax.ShapeDtypeStruct((n_rows, n_cols), BF16))
        out_specs.append(pl.BlockSpec((rows, n_cols),
                                      lambda i, last=last: (jnp.minimum(i, last), 0)))
    return in_specs, args, out_shape, out_specs


def _run_casts(srcs, dsts, cast_cols):
    for src, dst, cols in zip(srcs, dsts, cast_cols):
        if cols is None:
            dst[...] = src[...].astype(BF16)
        else:
            at = 0
            for start, width in cols:
                dst[:, at:at + width] = src[:, start:start + width].astype(BF16)
                at += width


def _qkv_a_kernel(x_ref, mod_ref, g_ref, w_ref, *refs, cast_cols):
    n_cast = len(cast_cols)
    cast_src, (q_ref, kv_ref), cast_dst = refs[:n_cast], refs[n_cast:n_cast + 2], refs[n_cast + 2:]
    _run_casts(cast_src, cast_dst, cast_cols)
    dq = q_ref.shape[1]
    rows = x_ref.shape[0] // QKV_ROW_SLICES
    for n in range(QKV_ROW_SLICES):
        rs = slice(n * rows, (n + 1) * rows)
        h = _norm_mod(x_ref[rs, :], g_ref[...], mod_ref[0:1, :], mod_ref[1:2, :])
        r = jnp.dot(h.astype(BF16), w_ref[...], preferred_element_type=F32)
        q_ref[rs, :] = (r[:, :dq] * Q_SCALE).astype(BF16)
        kv_ref[rs, :] = r[:, dq:].astype(BF16)


def _qkv_a(x, mod, g, w, casts=()):
    s, d = x.shape
    dq = A_Q_HEADS * HEAD_DIM
    dkv = w.arr.shape[2] - dq
    tm = 1024
    cast_in, cast_args, cast_shape, cast_out = _cast_specs(casts, s // tm)
    return pl.pallas_call(
        functools.partial(_qkv_a_kernel, cast_cols=tuple(cst.cols for cst in casts)),
        out_shape=(jax.ShapeDtypeStruct((s, dq), BF16),
                   jax.ShapeDtypeStruct((s, dkv), BF16), *cast_shape),
        grid=(s // tm,),
        in_specs=[
            pl.BlockSpec((tm, d), lambda i: (i, 0)),
            _layer_spec(mod),
            _layer_spec(g),
            _layer_spec(w),
            *cast_in,
        ],
        out_specs=(pl.BlockSpec((tm, dq), lambda i: (i, 0)),
                   pl.BlockSpec((tm, dkv), lambda i: (i, 0)), *cast_out),
        compiler_params=_params(),
        name="qkv_a",
    )(x, mod.arr, g.arr, w.arr, *cast_args)


B_DQ = B_Q_HEADS * HEAD_DIM
B_DKV = 2 * B_KV_HEADS * HEAD_DIM
B_DG = B_DQ + B_DKV


def _qkv_b_kernel(x_ref, mod_ref, g_ref, w_ref, *refs):
    out_refs, slab_ref = refs[:-1], refs[-1]
    tm = x_ref.shape[0]
    h = _norm_mod(x_ref[...], g_ref[...], mod_ref[0:1, :], mod_ref[1:2, :])
    hb = h.astype(BF16)
    nq = B_DQ // LANES
    for gi, (_, dil) in enumerate(B_GROUPS):
        q_ref, kv_ref = out_refs[2 * gi], out_refs[2 * gi + 1]
        r = jnp.dot(hb, w_ref[:, gi * B_DG:(gi + 1) * B_DG], preferred_element_type=F32)
        rq = r[:, :B_DQ] * Q_SCALE
        rkv = r[:, B_DQ:]
        if dil == 1:
            q_ref[0] = rq.astype(BF16)
            kv_ref[0] = rkv.astype(BF16)
            continue
        for c in range(B_DG // LANES):
            src = rq if c < nq else rkv
            cc = c if c < nq else c - nq
            slab_ref[c] = src[:, cc * LANES:(cc + 1) * LANES]
        rows = tm // dil
        for rr in range(dil):
            for c in range(B_DG // LANES):
                v = slab_ref[c, pl.ds(rr, rows, stride=dil), :].astype(BF16)
                if c < nq:
                    q_ref[rr, :, c * LANES:(c + 1) * LANES] = v
                else:
                    kv_ref[rr, :, (c - nq) * LANES:(c - nq + 1) * LANES] = v


def _qkv_b(x, mod, g, w):
    s, d = x.shape
    tm = 1024
    out_shape, out_specs = [], []
    for _, dil in B_GROUPS:
        for width in (B_DQ, B_DKV):
            out_shape.append(jax.ShapeDtypeStruct((dil, s // dil, width), BF16))
            out_specs.append(pl.BlockSpec((dil, tm // dil, width), lambda i: (0, i, 0)))
    return pl.pallas_call(
        _qkv_b_kernel,
        out_shape=tuple(out_shape),
        grid=(s // tm,),
        in_specs=[
            pl.BlockSpec((tm, d), lambda i: (i, 0)),
            _layer_spec(mod),
            _layer_spec(g),
            _layer_spec(w),
        ],
        out_specs=tuple(out_specs),
        scratch_shapes=[pltpu.VMEM((B_DG // LANES, tm, LANES), F32)],
        compiler_params=_params(),
        name="qkv_b",
    )(x, mod.arr, g.arr, w.arr)


def _attn_units(q_ref, kv_refs, sink_ref, o_ref, lse_ref, bufs, bias_ref, tile, first_step,
                *, n_kv, halo, seq_len, dist_scale, slopes, sink_row):
    kvp_ref, kvc_ref, kvn_ref = kv_refs
    has_sink = sink_row is not None
    emit_lse = lse_ref is not None
    ts = q_ref.shape[0]
    w = QBLK + 2 * halo
    nh = n_kv * GQ

    @pl.when(first_step)
    def _():
        kj = lax.broadcasted_iota(jnp.int32, (w, QBLK), 0)
        qi = lax.broadcasted_iota(jnp.int32, (w, QBLK), 1)
        rel = jnp.abs(kj - halo - qi)
        dist = (dist_scale * rel).astype(F32)
        for h, slope in enumerate(slopes):
            bias_ref[h] = jnp.where(rel <= halo, ((-float(slope)) * dist) * LOG2E, NEG)
        bias_ref[nh] = jnp.full((w, QBLK), NEG, F32)

    kvbuf, vt_ref = bufs
    kvbuf[0:halo, :] = kvp_ref[...]
    kvbuf[halo:halo + ts, :] = kvc_ref[...]
    kvbuf[halo + ts:, :] = kvn_ref[...]
    dkv = n_kv * HEAD_DIM
    for pair in range(n_kv // 2):
        vt_ref[pair * LANES:(pair + 1) * LANES, :] = (
            kvbuf[:, dkv + pair * LANES:dkv + (pair + 1) * LANES].T)

    ones = jnp.ones((ONES_COLS, w), BF16)
    nt_dims = (((1,), (1,)), ((), ()))

    def scores(r0, kvh):
        c0 = kvh * GQ * HEAD_DIM
        qg = q_ref[pl.ds(r0, QBLK), c0:c0 + GQ * HEAD_DIM]
        qs = jnp.concatenate(
            [qg[:, g * HEAD_DIM:(g + 1) * HEAD_DIM] for g in range(GQ)], axis=0)
        k = kvbuf[pl.ds(r0, w), kvh * HEAD_DIM:(kvh + 1) * HEAD_DIM]
        return lax.dot_general(k, qs, nt_dims, preferred_element_type=F32)

    def softmax(st, r0, kvh, gs):
        seq_pos = (tile * ts + r0) & (seq_len - 1)
        prev_out = seq_pos == 0
        next_out = seq_pos + QBLK == seq_len
        ps, ms = [], []
        for g in gs:
            h = kvh * GQ + g
            top = jnp.where(prev_out, nh, h)
            bot = jnp.where(next_out, nh, h)
            bias = jnp.concatenate([bias_ref[top, 0:halo, :],
                                    bias_ref[h, halo:halo + QBLK, :],
                                    bias_ref[bot, halo + QBLK:w, :]], axis=0)
            sg = st[:, g * QBLK:(g + 1) * QBLK] + bias
            m = jnp.max(sg, axis=0, keepdims=True)
            if has_sink:
                m = jnp.maximum(m, sink_ref[sink_row, h] * LOG2E)
            ps.append(jnp.exp2(sg - m).astype(BF16))
            ms.append(m)
        return jnp.concatenate(ps, axis=1), ms

    def weighted_values(pt, ms, r0, kvh, gs):
        vt = vt_ref[kvh * HEAD_DIM:(kvh + 1) * HEAD_DIM, r0:r0 + w]
        vt_ext = jnp.concatenate([vt, ones], axis=0)
        ot = jnp.dot(vt_ext, pt, preferred_element_type=F32)
        outs, lses = [], []
        for n, g in enumerate(gs):
            h = kvh * GQ + g
            cs = slice(n * QBLK, (n + 1) * QBLK)
            l = ot[HEAD_DIM:HEAD_DIM + 1, cs]
            if has_sink:
                l = l + jnp.exp2(sink_ref[sink_row, h] * LOG2E - ms[n])
            outs.append(ot[0:HEAD_DIM, cs] * (1.0 / l))
            if emit_lse:
                lses.append(jnp.broadcast_to(ms[n] * LN2 + jnp.log(l), (HEAD_DIM, QBLK)))
        for n in range(0, len(gs), 2):
            c0 = (kvh * GQ + gs[n]) * HEAD_DIM
            o2 = jnp.concatenate(outs[n:n + 2], axis=0)
            o_ref[r0:r0 + QBLK, c0:c0 + LANES] = o2.T.astype(o_ref.dtype)
            if emit_lse:
                l2 = jnp.concatenate(lses[n:n + 2], axis=0)
                lse_ref[r0:r0 + QBLK, c0:c0 + LANES] = l2.T

    units = [(qb * QBLK, kvh) for qb in range(ts // QBLK) for kvh in range(n_kv)]
    head_sets = [tuple(range(i, i + HEADS_PER_PV)) for i in range(0, GQ, HEADS_PER_PV)]
    pending, probs = {}, {}

    def issue(u):
        pending[u] = scores(*units[u])

    def soft(u):
        st = pending.pop(u)
        probs[u] = [softmax(st, *units[u], gs) for gs in head_sets]

    def finish(u):
        for gs, (pt, ms) in zip(head_sets, probs.pop(u)):
            weighted_values(pt, ms, *units[u], gs)

    return [(functools.partial(issue, u), functools.partial(soft, u), functools.partial(finish, u))
            for u in range(len(units))]


def _attn_kernel(*refs, emit_lse, pv_delay, **cfg):
    refs = list(refs)
    q_ref, kv_refs = refs[0], refs[1:4]
    pos = 4
    sink_ref = None
    if cfg["sink_row"] is not None:
        sink_ref = refs[pos]
        pos += 1
    o_ref = refs[pos]
    pos += 1
    lse_ref = None
    if emit_lse:
        lse_ref = refs[pos]
        pos += 1
    kvbufs, bias_ref = refs[pos:pos + 2], refs[pos + 2]
    step = pl.program_id(0)
    units = _attn_units(q_ref, kv_refs, sink_ref, o_ref, lse_ref, kvbufs, bias_ref,
                        step, step == 0, **cfg)
    units[0][0]()
    for u, (_, soft, _) in enumerate(units):
        if u + 1 < len(units):
            units[u + 1][0]()
        soft()
        if u >= pv_delay:
            units[u - pv_delay][2]()
    for u in range(max(len(units) - pv_delay, 0), len(units)):
        units[u][2]()


def _attn_in_specs(s, ts, dq, dkv, halo):
    per = ts // halo
    last_halo = s // halo - 1
    return [
        pl.BlockSpec((ts, dq), lambda i: (i, 0)),
        pl.BlockSpec((halo, dkv), lambda i: (jnp.maximum(i * per - 1, 0), 0)),
        pl.BlockSpec((ts, dkv), lambda i: (i, 0)),
        pl.BlockSpec((halo, dkv), lambda i: (jnp.minimum((i + 1) * per, last_halo), 0)),
    ]


def _attn_scratch(ts, dkv, n_kv, halo):
    return [pltpu.VMEM((ts + 2 * halo, dkv), BF16),
            pltpu.VMEM((n_kv * HEAD_DIM, ts + 2 * halo), BF16),
            pltpu.VMEM((n_kv * GQ + 1, QBLK + 2 * halo, QBLK), F32)]


def _attention(q, kv, *, n_kv, halo, seq_len, dist_scale, slopes, pv_delay, units_per_step,
               sink=None, emit_lse=False, out_dtype=BF16):
    s, dq = q.shape
    dkv = kv.shape[1]
    ts = min(QBLK * (units_per_step // n_kv), seq_len)
    assert seq_len % ts == 0 and ts % halo == 0 and dq == n_kv * GQ * HEAD_DIM
    assert dkv == 2 * n_kv * HEAD_DIM
    in_specs = _attn_in_specs(s, ts, dq, dkv, halo)
    args = [q, kv, kv, kv]
    if sink is not None:
        in_specs.append(pl.BlockSpec(memory_space=pltpu.SMEM))
        args.append(sink.arr)
    out_shape = [jax.ShapeDtypeStruct((s, dq), out_dtype)]
    out_specs = [pl.BlockSpec((ts, dq), lambda i: (i, 0))]
    if emit_lse:
        out_shape.append(jax.ShapeDtypeStruct((s, dq), F32))
        out_specs.append(pl.BlockSpec((ts, dq), lambda i: (i, 0)))
    kern = functools.partial(
        _attn_kernel, emit_lse=emit_lse, pv_delay=pv_delay, n_kv=n_kv, halo=halo, seq_len=seq_len,
        dist_scale=dist_scale, slopes=tuple(float(x) for x in slopes),
        sink_row=None if sink is None else sink.idx)
    return pl.pallas_call(
        kern,
        out_shape=tuple(out_shape),
        grid=(s // ts,),
        in_specs=in_specs,
        out_specs=tuple(out_specs),
        scratch_shapes=_attn_scratch(ts, dkv, n_kv, halo),
        compiler_params=_params(),
        name="banded_attn",
    )(*args)


def _silu(x):
    return x * jax.nn.sigmoid(x)


def _post_tail(get_o, x_ref, mod_ref, g_ref, wo_ref, wi_ref, wo2_ref, fin_ref, out_ref):
    tm = x_ref.shape[0]
    rows = tm // POST_ROW_SLICES
    dff = wo2_ref.shape[0]
    chunks = ((0, FFN_SPLIT), (FFN_SPLIT, dff))

    def gate_up(st, c):
        lo, hi = chunks[c]
        st["gu"] = (jnp.dot(st["h2"], wi_ref[:, lo:hi], preferred_element_type=F32),
                    jnp.dot(st["h2"], wi_ref[:, dff + lo:dff + hi], preferred_element_type=F32))

    def down(st, c):
        lo, hi = chunks[c]
        gate, up = st["gu_prev"]
        act = (_silu(gate) * up).astype(BF16)
        part = jnp.dot(act, wo2_ref[lo:hi, :], preferred_element_type=F32)
        st["acc"] = part if c == 0 else st["acc"] + part

    def stage(st, k):
        if k == 0:
            st["y"] = jnp.dot(get_o(st["rs"]), wo_ref[...], preferred_element_type=F32)
        elif k == 1:
            st["x1"] = x_ref[st["rs"], :] + mod_ref[2:3, :] * st["y"]
            st["h2"] = _norm_mod(st["x1"], g_ref[...], mod_ref[3:4, :],
                                 mod_ref[4:5, :]).astype(BF16)
            gate_up(st, 0)
        elif k <= len(chunks):
            st["gu_prev"] = st["gu"]
            gate_up(st, k - 1)
            down(st, k - 2)
        else:
            st["gu_prev"] = st["gu"]
            down(st, len(chunks) - 1)
            x2 = st["x1"] + mod_ref[5:6, :] * st["acc"]
            if fin_ref is not None:
                ms = jnp.mean(x2 * x2, axis=-1, keepdims=True)
                x2 = (x2 * lax.rsqrt(ms + RMS_EPS)) * fin_ref[...]
            out_ref[st["rs"], :] = x2

    states = [{"rs": slice(n * rows, (n + 1) * rows)} for n in range(POST_ROW_SLICES)]
    for k in range(len(chunks) + 2):
        for st in states:
            stage(st, k)


def _merged_o(attn_refs, slabs, om_ref):
    o0_ref, l0_ref, o1_ref, l1_ref, o2_ref, l2_ref = attn_refs
    so1, sl1, so2, sl2 = slabs
    nc = B_DQ // LANES

    def merged(rs):
        for src_o, src_l, dst_o, dst_l in ((o1_ref, l1_ref, so1, sl1), (o2_ref, l2_ref, so2, sl2)):
            dil = src_o.shape[0]
            n_src = (rs.stop - rs.start) // dil
            src_rs = slice(rs.start // dil, rs.start // dil + n_src)
            for rr in range(dil):
                dst_rows = pl.ds(rs.start + rr, n_src, stride=dil)
                for c in range(nc):
                    cs = slice(c * LANES, (c + 1) * LANES)
                    dst_o[c, dst_rows, :] = src_o[rr, src_rs, cs]
                    dst_l[c, dst_rows, :] = src_l[rr, src_rs, cs]
        for c in range(nc):
            cs = slice(c * LANES, (c + 1) * LANES)
            l0, l1, l2 = l0_ref[0, rs, cs], sl1[c, rs, :], sl2[c, rs, :]
            mx = jnp.maximum(jnp.maximum(l0, l1), l2)
            w0, w1, w2 = jnp.exp(l0 - mx), jnp.exp(l1 - mx), jnp.exp(l2 - mx)
            num = w0 * o0_ref[0, rs, cs] + w1 * so1[c, rs, :] + w2 * so2[c, rs, :]
            om_ref[rs, cs] = (num / (w0 + w1 + w2)).astype(BF16)
        return om_ref[rs, :]

    return merged


def _post_kernel(*refs, n_attn, final, cast_cols, has_ada):
    refs = list(refs)
    attn_refs = refs[:n_attn]
    x_ref, mod_ref, g_ref, wo_ref, wi_ref, wo2_ref = refs[n_attn:n_attn + 6]
    pos = n_attn + 6
    fin_ref = None
    if final:
        fin_ref = refs[pos]
        pos += 1
    n_cast = len(cast_cols)
    cast_src = refs[pos:pos + n_cast]
    pos += n_cast
    ada_in = refs[pos:pos + 3] if has_ada else None
    pos += 3 if has_ada else 0
    out_ref = refs[pos]
    cast_dst = refs[pos + 1:pos + 1 + n_cast]
    pos += 1 + n_cast
    ada_out = refs[pos] if has_ada else None
    pos += 1 if has_ada else 0
    scratch = refs[pos:]

    _run_casts(cast_src, cast_dst, cast_cols)
    if has_ada:
        _ada_block(*ada_in, ada_out)

    if n_attn == 1:
        (o_ref,) = attn_refs
        get_o = lambda rs: o_ref[rs, :]
    else:
        get_o = _merged_o(attn_refs, scratch[:4], scratch[4])
    _post_tail(get_o, x_ref, mod_ref, g_ref, wo_ref, wi_ref, wo2_ref, fin_ref, out_ref)


def _post(o_parts, x, mod, wo, g, wi, wo2, final_g, *, casts=(), next_ada=None):
    s, d = x.shape
    tm = 512
    steps = s // tm
    final = final_g is not None
    dilated = len(o_parts) > 1
    in_specs, args, scratch = [], [], []
    if dilated:
        for arr in o_parts:
            dil = arr.shape[0]
            in_specs.append(pl.BlockSpec((dil, tm // dil, arr.shape[2]), lambda i: (0, i, 0)))
            args.append(arr)
        scratch = [pltpu.VMEM((B_DQ // LANES, tm, LANES), F32) for _ in range(4)]
        scratch.append(pltpu.VMEM((tm, B_DQ), BF16))
    else:
        (o,) = o_parts
        in_specs.append(pl.BlockSpec((tm, o.shape[1]), lambda i: (i, 0)))
        args.append(o)
    params = [mod, g, wo, wi, wo2] + ([final_g] if final else [])
    in_specs += [pl.BlockSpec((tm, d), lambda i: (i, 0))] + [_layer_spec(p) for p in params]
    args += [x] + [p.arr for p in params]
    out_shape = [jax.ShapeDtypeStruct((s, d), F32)]
    out_specs = [pl.BlockSpec((tm, d), lambda i: (i, 0))]
    cast_in, cast_args, cast_shape, cast_out = _cast_specs(casts, steps)
    in_specs += cast_in
    args += cast_args
    out_shape += cast_shape
    out_specs += cast_out
    if next_ada is not None:
        c_col, ada_w, ada_b3, layer = next_ada
        n = ada_w.shape[2]
        n_blk = next(k for k in range(steps, 0, -1) if n % k == 0 and (n // k) % LANES == 0)
        tn, last = n // n_blk, n_blk - 1
        ada_in, ada_out = _ada_specs(ada_w, layer, tn, lambda i: jnp.minimum(i, last))
        in_specs += ada_in
        args += [c_col, ada_w, ada_b3]
        out_shape.append(jax.ShapeDtypeStruct((1, n), F32))
        out_specs.append(ada_out)
    kern = functools.partial(
        _post_kernel, n_attn=len(o_parts), final=final,
        cast_cols=tuple(cst.cols for cst in casts), has_ada=next_ada is not None)
    return pl.pallas_call(
        kern,
        out_shape=tuple(out_shape),
        grid=(steps,),
        in_specs=in_specs,
        out_specs=tuple(out_specs),
        scratch_shapes=scratch,
        compiler_params=_params(),
        name="post_b" if dilated else "post_a",
    )(*args)


def _b_w_in_cols():
    ng = len(B_GROUPS)
    dq, dk = ng * B_DQ, ng * B_KV_HEADS * HEAD_DIM
    dkg = B_KV_HEADS * HEAD_DIM
    cols = []
    for gi in range(ng):
        cols += [(gi * B_DQ, B_DQ), (dq + gi * dkg, dkg), (dq + dk + gi * dkg, dkg)]
    return tuple(cols)


def kernel(x, c, ada_w, ada_b, norm_mix, norm_ffn, ffn_w_in, ffn_w_out,
           a_w_in, a_w_out, a_sink, b_w_in, b_w_out, final_norm):
    b, s, d = x.shape
    assert b == 1
    depth = ada_w.shape[0]
    xs = x.reshape(s, d)
    c_col = c.reshape(d, 1)
    ada_b3 = ada_b.reshape(depth, 1, -1)
    g_mix = norm_mix.reshape(depth, 1, d)
    g_ffn = norm_ffn.reshape(depth, 1, d)
    slopes_a = _alibi_slopes(A_Q_HEADS)
    slopes_b = _alibi_slopes(len(B_GROUPS) * B_Q_HEADS)
    b_cols = _b_w_in_cols()

    def mixer_params(i):
        return (a_w_in, a_w_out, i // 2, None) if i % 2 == 0 else (b_w_in, b_w_out, i // 2, b_cols)

    mod = _ada_mod(c_col, ada_w, ada_b3, 0)
    w_in = _Layer(a_w_in[0].astype(BF16)[None], 0)
    casts0 = (_Cast(a_w_out, 0), _Cast(ffn_w_in, 0), _Cast(ffn_w_out, 0))
    for i in range(depth):
        mod_l = _Layer(mod.reshape(1, 6, d), 0)
        if i % 2 == 0:
            q, kv, *cast0 = _qkv_a(xs, mod_l, _Layer(g_mix, i), w_in, casts0 if i == 0 else ())
            if i == 0:
                w_out, wi, wo2 = (_Layer(r[None], 0) for r in cast0)
            o_parts = _attention(q, kv, n_kv=A_KV_HEADS, halo=A_HALF_WINDOW, seq_len=s,
                                 dist_scale=1, slopes=slopes_a, pv_delay=A_PV_DELAY,
                                 units_per_step=A_UNITS_PER_STEP,
                                 sink=_Layer(a_sink, i // 2))
        else:
            outs = _qkv_b(xs, mod_l, _Layer(g_mix, i), w_in)
            o_parts = []
            for gi, (window, dil) in enumerate(B_GROUPS):
                qg = outs[2 * gi].reshape(s, B_DQ)
                kvg = outs[2 * gi + 1].reshape(s, B_DKV)
                og, lg = _attention(
                    qg, kvg, n_kv=B_KV_HEADS, halo=window // (2 * dil), seq_len=s // dil,
                    dist_scale=dil, slopes=slopes_b[gi * B_Q_HEADS:(gi + 1) * B_Q_HEADS],
                    pv_delay=B_PV_DELAY, units_per_step=B_UNITS_PER_STEP,
                    emit_lse=True, out_dtype=F32)
                o_parts += [og.reshape(dil, s // dil, B_DQ), lg.reshape(dil, s // dil, B_DQ)]
        last = i == depth - 1
        casts, next_ada = (), None
        if not last:
            m_in, m_out, j, cols = mixer_params(i + 1)
            casts = (_Cast(m_in, j, cols), _Cast(m_out, j),
                     _Cast(ffn_w_in, i + 1), _Cast(ffn_w_out, i + 1))
            next_ada = (c_col, ada_w, ada_b3, i + 1)
        fin = _Layer(final_norm.reshape(1, 1, d), 0) if last else None
        res = _post(o_parts, xs, mod_l, w_out, _Layer(g_ffn, i), wi, wo2, fin,
                    casts=casts, next_ada=next_ada)
        xs = res[0]
        if not last:
            w_in, w_out, wi, wo2 = (_Layer(r[None], 0) for r in res[1:5])
            mod = res[5]
    return xs.reshape(b, s, d)
```

```python
import functools
from typing import NamedTuple

import numpy as np
import jax
import jax.numpy as jnp
from jax import lax
from jax.experimental import pallas as pl
from jax.experimental.pallas import tpu as pltpu

HEAD_DIM = 64
A_Q_HEADS = 16
A_KV_HEADS = 4
A_HALF_WINDOW = 128
B_GROUPS = ((128, 1), (512, 4), (2048, 16))
B_Q_HEADS = 8
B_KV_HEADS = 2
GQ = 4
RMS_EPS = 1e-6
NEG = -1e30
LOG2E = 1.4426950408889634
LN2 = 0.6931471805599453
Q_SCALE = HEAD_DIM ** -0.5 * LOG2E

LANES = 128
QBLK = 128
A_UNITS_PER_STEP, B_UNITS_PER_STEP = 64, 64
ONES_COLS = 32
HEADS_PER_PV = 4
A_PV_DELAY, B_PV_DELAY = 0, 1
FFN_SPLIT = 1536
QKV_ROW_SLICES = 2
POST_ROW_SLICES = 2
VMEM_LIMIT = 56 * 1024 * 1024

BF16 = jnp.bfloat16
F32 = jnp.float32


def _alibi_slopes(n):
    return np.asarray(2.0 ** (-8.0 * np.arange(1, n + 1) / n), dtype=np.float32)


class _Layer(NamedTuple):
    arr: jax.Array
    idx: int


def _layer_spec(p):
    nd = p.arr.ndim
    return pl.BlockSpec((None,) + p.arr.shape[1:], lambda *_: (p.idx,) + (0,) * (nd - 1),
                        pipeline_mode=pl.Buffered(1))


def _params(n_axes=1, **kwargs):
    return pltpu.CompilerParams(
        dimension_semantics=("arbitrary",) * n_axes, vmem_limit_bytes=VMEM_LIMIT, **kwargs)


def _ada_block(c_ref, w_ref, b_ref, o_ref):
    c = c_ref[...]
    cond = c * jax.nn.sigmoid(c)
    o_ref[...] = jnp.sum(w_ref[...] * cond, axis=0, keepdims=True) + b_ref[...]


def _ada_specs(ada_w, layer, tn, col_block):
    d = ada_w.shape[1]
    return ([pl.BlockSpec((d, 1), lambda *i: (0, 0)),
             pl.BlockSpec((None, d, tn), lambda *i: (layer, 0, col_block(*i))),
             pl.BlockSpec((None, 1, tn), lambda *i: (layer, 0, col_block(*i)))],
            pl.BlockSpec((1, tn), lambda *i: (0, col_block(*i))))


def _ada_mod(c_col, ada_w, ada_b3, layer):
    n = ada_w.shape[2]
    tn = 1536
    in_specs, out_spec = _ada_specs(ada_w, layer, tn, lambda j: j)
    return pl.pallas_call(
        _ada_block,
        out_shape=jax.ShapeDtypeStruct((1, n), F32),
        grid=(n // tn,),
        in_specs=in_specs,
        out_specs=out_spec,
        compiler_params=_params(),
        name="ada_mod",
    )(c_col, ada_w, ada_b3)


def _norm_mod(x, g, shift, scale):
    ms = jnp.mean(x * x, axis=-1, keepdims=True)
    y = x * lax.rsqrt(ms + RMS_EPS)
    return (y * g) * (1.0 + scale) + shift


class _Cast(NamedTuple):
    arr: jax.Array
    idx: int
    cols: tuple | None = None


BF16_SUBLANES = 16


def _cast_specs(casts, steps):
    in_specs, args, out_shape, out_specs = [], [], [], []
    for cst in casts:
        _, n_rows, n_cols = cst.arr.shape
        n_blk = next(n for n in range(steps, 0, -1)
                     if n_rows % n == 0 and (n_rows // n) % BF16_SUBLANES == 0)
        rows, last = n_rows // n_blk, n_blk - 1
        in_specs.append(pl.BlockSpec((None, rows, n_cols),
                                     lambda i, cst=cst, last=last: (cst.idx, jnp.minimum(i, last), 0)))
        args.append(cst.arr)
        out_shape.append(jax.ShapeDtypeStruct((n_rows, n_cols), BF16))
        out_specs.append(pl.BlockSpec((rows, n_cols),
                                      lambda i, last=last: (jnp.minimum(i, last), 0)))
    return in_specs, args, out_shape, out_specs


def _run_casts(srcs, dsts, cast_cols):
    for src, dst, cols in zip(srcs, dsts, cast_cols):
        if cols is None:
            dst[...] = src[...].astype(BF16)
        else:
            at = 0
            for start, width in cols:
                dst[:, at:at + width] = src[:, start:start + width].astype(BF16)
                at += width


def _qkv_a_kernel(x_ref, mod_ref, g_ref, w_ref, *refs, cast_cols):
    n_cast = len(cast_cols)
    cast_src, (q_ref, kv_ref), cast_dst = refs[:n_cast], refs[n_cast:n_cast + 2], refs[n_cast + 2:]
    _run_casts(cast_src, cast_dst, cast_cols)
    dq = q_ref.shape[1]
    rows = x_ref.shape[0] // QKV_ROW_SLICES
    for n in range(QKV_ROW_SLICES):
        rs = slice(n * rows, (n + 1) * rows)
        h = _norm_mod(x_ref[rs, :], g_ref[...], mod_ref[0:1, :], mod_ref[1:2, :])
        r = jnp.dot(h.astype(BF16), w_ref[...], preferred_element_type=F32)
        q_ref[rs, :] = (r[:, :dq] * Q_SCALE).astype(BF16)
        kv_ref[rs, :] = r[:, dq:].astype(BF16)


def _qkv_a(x, mod, g, w, casts=()):
    s, d = x.shape
    dq = A_Q_HEADS * HEAD_DIM
    dkv = w.arr.shape[2] - dq
    tm = 1024
    cast_in, cast_args, cast_shape, cast_out = _cast_specs(casts, s // tm)
    return pl.pallas_call(
        functools.partial(_qkv_a_kernel, cast_cols=tuple(cst.cols for cst in casts)),
        out_shape=(jax.ShapeDtypeStruct((s, dq), BF16),
                   jax.ShapeDtypeStruct((s, dkv), BF16), *cast_shape),
        grid=(s // tm,),
        in_specs=[
            pl.BlockSpec((tm, d), lambda i: (i, 0)),
            _layer_spec(mod),
            _layer_spec(g),
            _layer_spec(w),
            *cast_in,
        ],
        out_specs=(pl.BlockSpec((tm, dq), lambda i: (i, 0)),
                   pl.BlockSpec((tm, dkv), lambda i: (i, 0)), *cast_out),
        compiler_params=_params(),
        name="qkv_a",
    )(x, mod.arr, g.arr, w.arr, *cast_args)


B_DQ = B_Q_HEADS * HEAD_DIM
B_DKV = 2 * B_KV_HEADS * HEAD_DIM
B_DG = B_DQ + B_DKV


def _qkv_b_kernel(x_ref, mod_ref, g_ref, w_ref, *refs):
    out_refs, slab_ref = refs[:-1], refs[-1]
    tm = x_ref.shape[0]
    h = _norm_mod(x_ref[...], g_ref[...], mod_ref[0:1, :], mod_ref[1:2, :])
    hb = h.astype(BF16)
    nq = B_DQ // LANES
    for gi, (_, dil) in enumerate(B_GROUPS):
        q_ref, kv_ref = out_refs[2 * gi], out_refs[2 * gi + 1]
        r = jnp.dot(hb, w_ref[:, gi * B_DG:(gi + 1) * B_DG], preferred_element_type=F32)
        rq = r[:, :B_DQ] * Q_SCALE
        rkv = r[:, B_DQ:]
        if dil == 1:
            q_ref[0] = rq.astype(BF16)
            kv_ref[0] = rkv.astype(BF16)
            continue
        for c in range(B_DG // LANES):
            src = rq if c < nq else rkv
            cc = c if c < nq else c - nq
            slab_ref[c] = src[:, cc * LANES:(cc + 1) * LANES]
        rows = tm // dil
        for rr in range(dil):
            for c in range(B_DG // LANES):
                v = slab_ref[c, pl.ds(rr, rows, stride=dil), :].astype(BF16)
                if c < nq:
                    q_ref[rr, :, c * LANES:(c + 1) * LANES] = v
                else:
                    kv_ref[rr, :, (c - nq) * LANES:(c - nq + 1) * LANES] = v


def _qkv_b(x, mod, g, w):
    s, d = x.shape
    tm = 1024
    out_shape, out_specs = [], []
    for _, dil in B_GROUPS:
        for width in (B_DQ, B_DKV):
            out_shape.append(jax.ShapeDtypeStruct((dil, s // dil, width), BF16))
            out_specs.append(pl.BlockSpec((dil, tm // dil, width), lambda i: (0, i, 0)))
    return pl.pallas_call(
        _qkv_b_kernel,
        out_shape=tuple(out_shape),
        grid=(s // tm,),
        in_specs=[
            pl.BlockSpec((tm, d), lambda i: (i, 0)),
            _layer_spec(mod),
            _layer_spec(g),
            _layer_spec(w),
        ],
        out_specs=tuple(out_specs),
        scratch_shapes=[pltpu.VMEM((B_DG // LANES, tm, LANES), F32)],
        compiler_params=_params(),
        name="qkv_b",
    )(x, mod.arr, g.arr, w.arr)


def _attn_units(q_ref, kv_refs, sink_ref, o_ref, lse_ref, bufs, bias_ref, tile, first_step,
                *, n_kv, halo, seq_len, dist_scale, slopes, sink_row):
    kvp_ref, kvc_ref, kvn_ref = kv_refs
    has_sink = sink_row is not None
    emit_lse = lse_ref is not None
    ts = q_ref.shape[0]
    w = QBLK + 2 * halo
    nh = n_kv * GQ

    @pl.when(first_step)
    def _():
        kj = lax.broadcasted_iota(jnp.int32, (w, QBLK), 0)
        qi = lax.broadcasted_iota(jnp.int32, (w, QBLK), 1)
        rel = jnp.abs(kj - halo - qi)
        dist = (dist_scale * rel).astype(F32)
        for h, slope in enumerate(slopes):
            bias_ref[h] = jnp.where(rel <= halo, ((-float(slope)) * dist) * LOG2E, NEG)
        bias_ref[nh] = jnp.full((w, QBLK), NEG, F32)

    kvbuf, vt_ref = bufs
    kvbuf[0:halo, :] = kvp_ref[...]
    kvbuf[halo:halo + ts, :] = kvc_ref[...]
    kvbuf[halo + ts:, :] = kvn_ref[...]
    dkv = n_kv * HEAD_DIM
    for pair in range(n_kv // 2):
        vt_ref[pair * LANES:(pair + 1) * LANES, :] = (
            kvbuf[:, dkv + pair * LANES:dkv + (pair + 1) * LANES].T)

    ones = jnp.ones((ONES_COLS, w), BF16)
    nt_dims = (((1,), (1,)), ((), ()))

    def scores(r0, kvh):
        c0 = kvh * GQ * HEAD_DIM
        qg = q_ref[pl.ds(r0, QBLK), c0:c0 + GQ * HEAD_DIM]
        qs = jnp.concatenate(
            [qg[:, g * HEAD_DIM:(g + 1) * HEAD_DIM] for g in range(GQ)], axis=0)
        k = kvbuf[pl.ds(r0, w), kvh * HEAD_DIM:(kvh + 1) * HEAD_DIM]
        return lax.dot_general(k, qs, nt_dims, preferred_element_type=F32)

    def softmax(st, r0, kvh, gs):
        seq_pos = (tile * ts + r0) & (seq_len - 1)
        prev_out = seq_pos == 0
        next_out = seq_pos + QBLK == seq_len
        ps, ms = [], []
        for g in gs:
            h = kvh * GQ + g
            top = jnp.where(prev_out, nh, h)
            bot = jnp.where(next_out, nh, h)
            bias = jnp.concatenate([bias_ref[top, 0:halo, :],
                                    bias_ref[h, halo:halo + QBLK, :],
                                    bias_ref[bot, halo + QBLK:w, :]], axis=0)
            sg = st[:, g * QBLK:(g + 1) * QBLK] + bias
            m = jnp.max(sg, axis=0, keepdims=True)
            if has_sink:
                m = jnp.maximum(m, sink_ref[sink_row, h] * LOG2E)
            ps.append(jnp.exp2(sg - m).astype(BF16))
            ms.append(m)
        return jnp.concatenate(ps, axis=1), ms

    def weighted_values(pt, ms, r0, kvh, gs):
        vt = vt_ref[kvh * HEAD_DIM:(kvh + 1) * HEAD_DIM, r0:r0 + w]
        vt_ext = jnp.concatenate([vt, ones], axis=0)
        ot = jnp.dot(vt_ext, pt, preferred_element_type=F32)
        outs, lses = [], []
        for n, g in enumerate(gs):
            h = kvh * GQ + g
            cs = slice(n * QBLK, (n + 1) * QBLK)
            l = ot[HEAD_DIM:HEAD_DIM + 1, cs]
            if has_sink:
                l = l + jnp.exp2(sink_ref[sink_row, h] * LOG2E - ms[n])
            outs.append(ot[0:HEAD_DIM, cs] * (1.0 / l))
            if emit_lse:
                lses.append(jnp.broadcast_to(ms[n] * LN2 + jnp.log(l), (HEAD_DIM, QBLK)))
        for n in range(0, len(gs), 2):
            c0 = (kvh * GQ + gs[n]) * HEAD_DIM
            o2 = jnp.concatenate(outs[n:n + 2], axis=0)
            o_ref[r0:r0 + QBLK, c0:c0 + LANES] = o2.T.astype(o_ref.dtype)
            if emit_lse:
                l2 = jnp.concatenate(lses[n:n + 2], axis=0)
                lse_ref[r0:r0 + QBLK, c0:c0 + LANES] = l2.T

    units = [(qb * QBLK, kvh) for qb in range(ts // QBLK) for kvh in range(n_kv)]
    head_sets = [tuple(range(i, i + HEADS_PER_PV)) for i in range(0, GQ, HEADS_PER_PV)]
    pending, probs = {}, {}

    def issue(u):
        pending[u] = scores(*units[u])

    def soft(u):
        st = pending.pop(u)
        probs[u] = [softmax(st, *units[u], gs) for gs in head_sets]

    def finish(u):
        for gs, (pt, ms) in zip(head_sets, probs.pop(u)):
            weighted_values(pt, ms, *units[u], gs)

    return [(functools.partial(issue, u), functools.partial(soft, u), functools.partial(finish, u))
            for u in range(len(units))]


def _attn_kernel(*refs, emit_lse, pv_delay, **cfg):
    refs = list(refs)
    q_ref, kv_refs = refs[0], refs[1:4]
    pos = 4
    sink_ref = None
    if cfg["sink_row"] is not None:
        sink_ref = refs[pos]
        pos += 1
    o_ref = refs[pos]
    pos += 1
    lse_ref = None
    if emit_lse:
        lse_ref = refs[pos]
        pos += 1
    kvbufs, bias_ref = refs[pos:pos + 2], refs[pos + 2]
    step = pl.program_id(0)
    units = _attn_units(q_ref, kv_refs, sink_ref, o_ref, lse_ref, kvbufs, bias_ref,
                        step, step == 0, **cfg)
    units[0][0]()
    for u, (_, soft, _) in enumerate(units):
        if u + 1 < len(units):
            units[u + 1][0]()
        soft()
        if u >= pv_delay:
            units[u - pv_delay][2]()
    for u in range(max(len(units) - pv_delay, 0), len(units)):
        units[u][2]()


def _attn_in_specs(s, ts, dq, dkv, halo):
    per = ts // halo
    last_halo = s // halo - 1
    return [
        pl.BlockSpec((ts, dq), lambda i: (i, 0)),
        pl.BlockSpec((halo, dkv), lambda i: (jnp.maximum(i * per - 1, 0), 0)),
        pl.BlockSpec((ts, dkv), lambda i: (i, 0)),
        pl.BlockSpec((halo, dkv), lambda i: (jnp.minimum((i + 1) * per, last_halo), 0)),
    ]


def _attn_scratch(ts, dkv, n_kv, halo):
    return [pltpu.VMEM((ts + 2 * halo, dkv), BF16),
            pltpu.VMEM((n_kv * HEAD_DIM, ts + 2 * halo), BF16),
            pltpu.VMEM((n_kv * GQ + 1, QBLK + 2 * halo, QBLK), F32)]


def _attention(q, kv, *, n_kv, halo, seq_len, dist_scale, slopes, pv_delay, units_per_step,
               sink=None, emit_lse=False, out_dtype=BF16):
    s, dq = q.shape
    dkv = kv.shape[1]
    ts = QBLK * (units_per_step // n_kv)
    assert s % ts == 0 and (ts % seq_len == 0 or seq_len % ts == 0)
    assert ts % halo == 0 and dq == n_kv * GQ * HEAD_DIM
    assert dkv == 2 * n_kv * HEAD_DIM
    in_specs = _attn_in_specs(s, ts, dq, dkv, halo)
    args = [q, kv, kv, kv]
    if sink is not None:
        in_specs.append(pl.BlockSpec(memory_space=pltpu.SMEM))
        args.append(sink.arr)
    out_shape = [jax.ShapeDtypeStruct((s, dq), out_dtype)]
    out_specs = [pl.BlockSpec((ts, dq), lambda i: (i, 0))]
    if emit_lse:
        out_shape.append(jax.ShapeDtypeStruct((s, dq), F32))
        out_specs.append(pl.BlockSpec((ts, dq), lambda i: (i, 0)))
    kern = functools.partial(
        _attn_kernel, emit_lse=emit_lse, pv_delay=pv_delay, n_kv=n_kv, halo=halo, seq_len=seq_len,
        dist_scale=dist_scale, slopes=tuple(float(x) for x in slopes),
        sink_row=None if sink is None else sink.idx)
    return pl.pallas_call(
        kern,
        out_shape=tuple(out_shape),
        grid=(s // ts,),
        in_specs=in_specs,
        out_specs=tuple(out_specs),
        scratch_shapes=_attn_scratch(ts, dkv, n_kv, halo),
        compiler_params=_params(),
        name="banded_attn",
    )(*args)


def _silu(x):
    return x * jax.nn.sigmoid(x)


def _post_tail(get_o, x_ref, mod_ref, g_ref, wo_ref, wi_ref, wo2_ref, fin_ref, out_ref):
    tm = x_ref.shape[0]
    rows = tm // POST_ROW_SLICES
    dff = wo2_ref.shape[0]
    chunks = ((0, FFN_SPLIT), (FFN_SPLIT, dff))

    def gate_up(st, c):
        lo, hi = chunks[c]
        st["gu"] = (jnp.dot(st["h2"], wi_ref[:, lo:hi], preferred_element_type=F32),
                    jnp.dot(st["h2"], wi_ref[:, dff + lo:dff + hi], preferred_element_type=F32))

    def down(st, c):
        lo, hi = chunks[c]
        gate, up = st["gu_prev"]
        act = (_silu(gate) * up).astype(BF16)
        part = jnp.dot(act, wo2_ref[lo:hi, :], preferred_element_type=F32)
        st["acc"] = part if c == 0 else st["acc"] + part

    def stage(st, k):
        if k == 0:
            st["y"] = jnp.dot(get_o(st["rs"]), wo_ref[...], preferred_element_type=F32)
        elif k == 1:
            st["x1"] = x_ref[st["rs"], :] + mod_ref[2:3, :] * st["y"]
            st["h2"] = _norm_mod(st["x1"], g_ref[...], mod_ref[3:4, :],
                                 mod_ref[4:5, :]).astype(BF16)
            gate_up(st, 0)
        elif k <= len(chunks):
            st["gu_prev"] = st["gu"]
            gate_up(st, k - 1)
            down(st, k - 2)
        else:
            st["gu_prev"] = st["gu"]
            down(st, len(chunks) - 1)
            x2 = st["x1"] + mod_ref[5:6, :] * st["acc"]
            if fin_ref is not None:
                ms = jnp.mean(x2 * x2, axis=-1, keepdims=True)
                x2 = (x2 * lax.rsqrt(ms + RMS_EPS)) * fin_ref[...]
            out_ref[st["rs"], :] = x2

    states = [{"rs": slice(n * rows, (n + 1) * rows)} for n in range(POST_ROW_SLICES)]
    for k in range(len(chunks) + 2):
        for st in states:
            stage(st, k)


def _merged_o(attn_refs, slabs, om_ref):
    o0_ref, l0_ref, o1_ref, l1_ref, o2_ref, l2_ref = attn_refs
    so1, sl1, so2, sl2 = slabs
    nc = B_DQ // LANES

    def merged(rs):
        for src_o, src_l, dst_o, dst_l in ((o1_ref, l1_ref, so1, sl1), (o2_ref, l2_ref, so2, sl2)):
            dil = src_o.shape[0]
            n_src = (rs.stop - rs.start) // dil
            src_rs = slice(rs.start // dil, rs.start // dil + n_src)
            for rr in range(dil):
                dst_rows = pl.ds(rs.start + rr, n_src, stride=dil)
                for c in range(nc):
                    cs = slice(c * LANES, (c + 1) * LANES)
                    dst_o[c, dst_rows, :] = src_o[rr, src_rs, cs]
                    dst_l[c, dst_rows, :] = src_l[rr, src_rs, cs]
        for c in range(nc):
            cs = slice(c * LANES, (c + 1) * LANES)
            l0, l1, l2 = l0_ref[0, rs, cs], sl1[c, rs, :], sl2[c, rs, :]
            mx = jnp.maximum(jnp.maximum(l0, l1), l2)
            w0, w1, w2 = jnp.exp(l0 - mx), jnp.exp(l1 - mx), jnp.exp(l2 - mx)
            num = w0 * o0_ref[0, rs, cs] + w1 * so1[c, rs, :] + w2 * so2[c, rs, :]
            om_ref[rs, cs] = (num / (w0 + w1 + w2)).astype(BF16)
        return om_ref[rs, :]

    return merged


def _post_kernel(*refs, n_attn, final, cast_cols, has_ada):
    refs = list(refs)
    attn_refs = refs[:n_attn]
    x_ref, mod_ref, g_ref, wo_ref, wi_ref, wo2_ref = refs[n_attn:n_attn + 6]
    pos = n_attn + 6
    fin_ref = None
    if final:
        fin_ref = refs[pos]
        pos += 1
    n_cast = len(cast_cols)
    cast_src = refs[pos:pos + n_cast]
    pos += n_cast
    ada_in = refs[pos:pos + 3] if has_ada else None
    pos += 3 if has_ada else 0
    out_ref = refs[pos]
    cast_dst = refs[pos + 1:pos + 1 + n_cast]
    pos += 1 + n_cast
    ada_out = refs[pos] if has_ada else None
    pos += 1 if has_ada else 0
    scratch = refs[pos:]

    _run_casts(cast_src, cast_dst, cast_cols)
    if has_ada:
        _ada_block(*ada_in, ada_out)

    if n_attn == 1:
        (o_ref,) = attn_refs
        get_o = lambda rs: o_ref[rs, :]
    else:
        get_o = _merged_o(attn_refs, scratch[:4], scratch[4])
    _post_tail(get_o, x_ref, mod_ref, g_ref, wo_ref, wi_ref, wo2_ref, fin_ref, out_ref)


def _post(o_parts, x, mod, wo, g, wi, wo2, final_g, *, casts=(), next_ada=None):
    s, d = x.shape
    tm = 512
    steps = s // tm
    final = final_g is not None
    dilated = len(o_parts) > 1
    in_specs, args, scratch = [], [], []
    if dilated:
        for arr in o_parts:
            dil = arr.shape[0]
            in_specs.append(pl.BlockSpec((dil, tm // dil, arr.shape[2]), lambda i: (0, i, 0)))
            args.append(arr)
        scratch = [pltpu.VMEM((B_DQ // LANES, tm, LANES), F32) for _ in range(4)]
        scratch.append(pltpu.VMEM((tm, B_DQ), BF16))
    else:
        (o,) = o_parts
        in_specs.append(pl.BlockSpec((tm, o.shape[1]), lambda i: (i, 0)))
        args.append(o)
    params = [mod, g, wo, wi, wo2] + ([final_g] if final else [])
    in_specs += [pl.BlockSpec((tm, d), lambda i: (i, 0))] + [_layer_spec(p) for p in params]
    args += [x] + [p.arr for p in params]
    out_shape = [jax.ShapeDtypeStruct((s, d), F32)]
    out_specs = [pl.BlockSpec((tm, d), lambda i: (i, 0))]
    cast_in, cast_args, cast_shape, cast_out = _cast_specs(casts, steps)
    in_specs += cast_in
    args += cast_args
    out_shape += cast_shape
    out_specs += cast_out
    if next_ada is not None:
        c_col, ada_w, ada_b3, layer = next_ada
        n = ada_w.shape[2]
        n_blk = next(k for k in range(steps, 0, -1) if n % k == 0 and (n // k) % LANES == 0)
        tn, last = n // n_blk, n_blk - 1
        ada_in, ada_out = _ada_specs(ada_w, layer, tn, lambda i: jnp.minimum(i, last))
        in_specs += ada_in
        args += [c_col, ada_w, ada_b3]
        out_shape.append(jax.ShapeDtypeStruct((1, n), F32))
        out_specs.append(ada_out)
    kern = functools.partial(
        _post_kernel, n_attn=len(o_parts), final=final,
        cast_cols=tuple(cst.cols for cst in casts), has_ada=next_ada is not None)
    return pl.pallas_call(
        kern,
        out_shape=tuple(out_shape),
        grid=(steps,),
        in_specs=in_specs,
        out_specs=tuple(out_specs),
        scratch_shapes=scratch,
        compiler_params=_params(),
        name="post_b" if dilated else "post_a",
    )(*args)


def _b_w_in_cols():
    ng = len(B_GROUPS)
    dq, dk = ng * B_DQ, ng * B_KV_HEADS * HEAD_DIM
    dkg = B_KV_HEADS * HEAD_DIM
    cols = []
    for gi in range(ng):
        cols += [(gi * B_DQ, B_DQ), (dq + gi * dkg, dkg), (dq + dk + gi * dkg, dkg)]
    return tuple(cols)


def kernel(x, c, ada_w, ada_b, norm_mix, norm_ffn, ffn_w_in, ffn_w_out,
           a_w_in, a_w_out, a_sink, b_w_in, b_w_out, final_norm):
    b, s, d = x.shape
    assert b == 1
    depth = ada_w.shape[0]
    xs = x.reshape(s, d)
    c_col = c.reshape(d, 1)
    ada_b3 = ada_b.reshape(depth, 1, -1)
    g_mix = norm_mix.reshape(depth, 1, d)
    g_ffn = norm_ffn.reshape(depth, 1, d)
    slopes_a = _alibi_slopes(A_Q_HEADS)
    slopes_b = _alibi_slopes(len(B_GROUPS) * B_Q_HEADS)
    b_cols = _b_w_in_cols()

    def mixer_params(i):
        return (a_w_in, a_w_out, i // 2, None) if i % 2 == 0 else (b_w_in, b_w_out, i // 2, b_cols)

    mod = _ada_mod(c_col, ada_w, ada_b3, 0)
    w_in = _Layer(a_w_in[0].astype(BF16)[None], 0)
    casts0 = (_Cast(a_w_out, 0), _Cast(ffn_w_in, 0), _Cast(ffn_w_out, 0))
    for i in range(depth):
        mod_l = _Layer(mod.reshape(1, 6, d), 0)
        if i % 2 == 0:
            q, kv, *cast0 = _qkv_a(xs, mod_l, _Layer(g_mix, i), w_in, casts0 if i == 0 else ())
            if i == 0:
                w_out, wi, wo2 = (_Layer(r[None], 0) for r in cast0)
            o_parts = _attention(q, kv, n_kv=A_KV_HEADS, halo=A_HALF_WINDOW, seq_len=s,
                                 dist_scale=1, slopes=slopes_a, pv_delay=A_PV_DELAY,
                                 units_per_step=A_UNITS_PER_STEP,
                                 sink=_Layer(a_sink, i // 2))
        else:
            outs = _qkv_b(xs, mod_l, _Layer(g_mix, i), w_in)
            o_parts = []
            for gi, (window, dil) in enumerate(B_GROUPS):
                qg = outs[2 * gi].reshape(s, B_DQ)
                kvg = outs[2 * gi + 1].reshape(s, B_DKV)
                og, lg = _attention(
                    qg, kvg, n_kv=B_KV_HEADS, halo=window // (2 * dil), seq_len=s // dil,
                    dist_scale=dil, slopes=slopes_b[gi * B_Q_HEADS:(gi + 1) * B_Q_HEADS],
                    pv_delay=B_PV_DELAY, units_per_step=B_UNITS_PER_STEP,
                    emit_lse=True, out_dtype=F32)
                o_parts += [og.reshape(dil, s // dil, B_DQ), lg.reshape(dil, s // dil, B_DQ)]
        last = i == depth - 1
        casts, next_ada = (), None
        if not last:
            m_in, m_out, j, cols = mixer_params(i + 1)
            casts = (_Cast(m_in, j, cols), _Cast(m_out, j),
                     _Cast(ffn_w_in, i + 1), _Cast(ffn_w_out, i + 1))
            next_ada = (c_col, ada_w, ada_b3, i + 1)
        fin = _Layer(final_norm.reshape(1, 1, d), 0) if last else None
        res = _post(o_parts, xs, mod_l, w_out, _Layer(g_ffn, i), wi, wo2, fin,
                    casts=casts, next_ada=next_ada)
        xs = res[0]
        if not last:
            w_in, w_out, wi, wo2 = (_Layer(r[None], 0) for r in res[1:5])
            mod = res[5]
    return xs.reshape(b, s, d)
```

```python
import functools
from typing import NamedTuple

import numpy as np
import jax
import jax.numpy as jnp
from jax import lax
from jax.experimental import pallas as pl
from jax.experimental.pallas import tpu as pltpu

HEAD_DIM = 64
A_Q_HEADS = 16
A_KV_HEADS = 4
A_HALF_WINDOW = 128
B_GROUPS = ((128, 1), (512, 4), (2048, 16))
B_Q_HEADS = 8
B_KV_HEADS = 2
GQ = 4
RMS_EPS = 1e-6
NEG = -1e30
LOG2E = 1.4426950408889634
LN2 = 0.6931471805599453
Q_SCALE = HEAD_DIM ** -0.5 * LOG2E

LANES = 128
QBLK = 128
A_UNITS_PER_STEP, B_UNITS_PER_STEP = 64, 64
ONES_COLS = 32
HEADS_PER_PV = 4
A_PV_DELAY, B_PV_DELAY = 0, 1
FFN_SPLIT = 1536
QKV_ROW_SLICES = 2
POST_ROW_SLICES = 2
VMEM_LIMIT = 56 * 1024 * 1024

BF16 = jnp.bfloat16
F32 = jnp.float32


def _alibi_slopes(n):
    return np.asarray(2.0 ** (-8.0 * np.arange(1, n + 1) / n), dtype=np.float32)


class _Layer(NamedTuple):
    arr: jax.Array
    idx: int


def _layer_spec(p):
    nd = p.arr.ndim
    return pl.BlockSpec((None,) + p.arr.shape[1:], lambda *_: (p.idx,) + (0,) * (nd - 1),
                        pipeline_mode=pl.Buffered(1))


def _params(n_axes=1, **kwargs):
    return pltpu.CompilerParams(
        dimension_semantics=("arbitrary",) * n_axes, vmem_limit_bytes=VMEM_LIMIT, **kwargs)


def _ada_block(c_ref, w_ref, b_ref, o_ref):
    c = c_ref[...]
    cond = c * jax.nn.sigmoid(c)
    o_ref[...] = jnp.sum(w_ref[...] * cond, axis=0, keepdims=True) + b_ref[...]


def _ada_specs(ada_w, layer, tn, col_block):
    d = ada_w.shape[1]
    return ([pl.BlockSpec((d, 1), lambda *i: (0, 0)),
             pl.BlockSpec((None, d, tn), lambda *i: (layer, 0, col_block(*i))),
             pl.BlockSpec((None, 1, tn), lambda *i: (layer, 0, col_block(*i)))],
            pl.BlockSpec((1, tn), lambda *i: (0, col_block(*i))))


def _ada_mod(c_col, ada_w, ada_b3, layer):
    n = ada_w.shape[2]
    tn = 1536
    in_specs, out_spec = _ada_specs(ada_w, layer, tn, lambda j: j)
    return pl.pallas_call(
        _ada_block,
        out_shape=jax.ShapeDtypeStruct((1, n), F32),
        grid=(n // tn,),
        in_specs=in_specs,
        out_specs=out_spec,
        compiler_params=_params(),
        name="ada_mod",
    )(c_col, ada_w, ada_b3)


def _norm_mod(x, g, shift, scale):
    ms = jnp.mean(x * x, axis=-1, keepdims=True)
    y = x * lax.rsqrt(ms + RMS_EPS)
    return (y * g) * (1.0 + scale) + shift


class _Cast(NamedTuple):
    arr: jax.Array
    idx: int
    cols: tuple | None = None


BF16_SUBLANES = 16


def _cast_specs(casts, steps):
    in_specs, args, out_shape, out_specs = [], [], [], []
    for cst in casts:
        _, n_rows, n_cols = cst.arr.shape
        n_blk = next(n for n in range(steps, 0, -1)
                     if n_rows % n == 0 and (n_rows // n) % BF16_SUBLANES == 0)
        rows, last = n_rows // n_blk, n_blk - 1
        in_specs.append(pl.BlockSpec((None, rows, n_cols),
                                     lambda i, cst=cst, last=last: (cst.idx, jnp.minimum(i, last), 0)))
        args.append(cst.arr)
        out_shape.append(jax.ShapeDtypeStruct((n_rows, n_cols), BF16))
        out_specs.append(pl.BlockSpec((rows, n_cols),
                                      lambda i, last=last: (jnp.minimum(i, last), 0)))
    return in_specs, args, out_shape, out_specs


def _run_casts(srcs, dsts, cast_cols):
    for src, dst, cols in zip(srcs, dsts, cast_cols):
        if cols is None:
            dst[...] = src[...].astype(BF16)
        else:
            at = 0
            for start, width in cols:
                dst[:, at:at + width] = src[:, start:start + width].astype(BF16)
                at += width


def _qkv_a_kernel(x_ref, mod_ref, g_ref, w_ref, *refs, cast_cols):
    n_cast = len(cast_cols)
    cast_src, (q_ref, kv_ref), cast_dst = refs[:n_cast], refs[n_cast:n_cast + 2], refs[n_cast + 2:]
    _run_casts(cast_src, cast_dst, cast_cols)
    dq = q_ref.shape[1]
    rows = x_ref.shape[0] // QKV_ROW_SLICES
    for n in range(QKV_ROW_SLICES):
        rs = slice(n * rows, (n + 1) * rows)
        h = _norm_mod(x_ref[rs, :], g_ref[...], mod_ref[0:1, :], mod_ref[1:2, :])
        r = jnp.dot(h.astype(BF16), w_ref[...], preferred_element_type=F32)
        q_ref[rs, :] = (r[:, :dq] * Q_SCALE).astype(BF16)
        kv_ref[rs, :] = r[:, dq:].astype(BF16)


def _qkv_a(x, mod, g, w, casts=()):
    s, d = x.shape
    dq = A_Q_HEADS * HEAD_DIM
    dkv = w.arr.shape[2] - dq
    tm = 2048
    cast_in, cast_args, cast_shape, cast_out = _cast_specs(casts, s // tm)
    return pl.pallas_call(
        functools.partial(_qkv_a_kernel, cast_cols=tuple(cst.cols for cst in casts)),
        out_shape=(jax.ShapeDtypeStruct((s, dq), BF16),
                   jax.ShapeDtypeStruct((s, dkv), BF16), *cast_shape),
        grid=(s // tm,),
        in_specs=[
            pl.BlockSpec((tm, d), lambda i: (i, 0)),
            _layer_spec(mod),
            _layer_spec(g),
            _layer_spec(w),
            *cast_in,
        ],
        out_specs=(pl.BlockSpec((tm, dq), lambda i: (i, 0)),
                   pl.BlockSpec((tm, dkv), lambda i: (i, 0)), *cast_out),
        compiler_params=_params(),
        name="qkv_a",
    )(x, mod.arr, g.arr, w.arr, *cast_args)


B_DQ = B_Q_HEADS * HEAD_DIM
B_DKV = 2 * B_KV_HEADS * HEAD_DIM
B_DG = B_DQ + B_DKV


def _qkv_b_kernel(x_ref, mod_ref, g_ref, w_ref, *refs):
    out_refs, slab_ref = refs[:-1], refs[-1]
    tm = x_ref.shape[0]
    h = _norm_mod(x_ref[...], g_ref[...], mod_ref[0:1, :], mod_ref[1:2, :])
    hb = h.astype(BF16)
    nq = B_DQ // LANES
    for gi, (_, dil) in enumerate(B_GROUPS):
        q_ref, kv_ref = out_refs[2 * gi], out_refs[2 * gi + 1]
        r = jnp.dot(hb, w_ref[:, gi * B_DG:(gi + 1) * B_DG], preferred_element_type=F32)
        rq = r[:, :B_DQ] * Q_SCALE
        rkv = r[:, B_DQ:]
        if dil == 1:
            q_ref[0] = rq.astype(BF16)
            kv_ref[0] = rkv.astype(BF16)
            continue
        for c in range(B_DG // LANES):
            src = rq if c < nq else rkv
            cc = c if c < nq else c - nq
            slab_ref[c] = src[:, cc * LANES:(cc + 1) * LANES]
        rows = tm // dil
        for rr in range(dil):
            for c in range(B_DG // LANES):
                v = slab_ref[c, pl.ds(rr, rows, stride=dil), :].astype(BF16)
                if c < nq:
                    q_ref[rr, :, c * LANES:(c + 1) * LANES] = v
                else:
                    kv_ref[rr, :, (c - nq) * LANES:(c - nq + 1) * LANES] = v


def _qkv_b(x, mod, g, w):
    s, d = x.shape
    tm = 1024
    out_shape, out_specs = [], []
    for _, dil in B_GROUPS:
        for width in (B_DQ, B_DKV):
            out_shape.append(jax.ShapeDtypeStruct((dil, s // dil, width), BF16))
            out_specs.append(pl.BlockSpec((dil, tm // dil, width), lambda i: (0, i, 0)))
    return pl.pallas_call(
        _qkv_b_kernel,
        out_shape=tuple(out_shape),
        grid=(s // tm,),
        in_specs=[
            pl.BlockSpec((tm, d), lambda i: (i, 0)),
            _layer_spec(mod),
            _layer_spec(g),
            _layer_spec(w),
        ],
        out_specs=tuple(out_specs),
        scratch_shapes=[pltpu.VMEM((B_DG // LANES, tm, LANES), F32)],
        compiler_params=_params(),
        name="qkv_b",
    )(x, mod.arr, g.arr, w.arr)


def _attn_units(q_ref, kv_refs, sink_ref, o_ref, lse_ref, bufs, bias_ref, tile, first_step,
                *, n_kv, halo, seq_len, dist_scale, slopes, sink_row):
    kvp_ref, kvc_ref, kvn_ref = kv_refs
    has_sink = sink_row is not None
    emit_lse = lse_ref is not None
    ts = q_ref.shape[0]
    w = QBLK + 2 * halo
    nh = n_kv * GQ

    @pl.when(first_step)
    def _():
        kj = lax.broadcasted_iota(jnp.int32, (w, QBLK), 0)
        qi = lax.broadcasted_iota(jnp.int32, (w, QBLK), 1)
        rel = jnp.abs(kj - halo - qi)
        dist = (dist_scale * rel).astype(F32)
        for h, slope in enumerate(slopes):
            bias_ref[h] = jnp.where(rel <= halo, ((-float(slope)) * dist) * LOG2E, NEG)
        bias_ref[nh] = jnp.full((w, QBLK), NEG, F32)

    kvbuf, vt_ref = bufs
    kvbuf[0:halo, :] = kvp_ref[...]
    kvbuf[halo:halo + ts, :] = kvc_ref[...]
    kvbuf[halo + ts:, :] = kvn_ref[...]
    dkv = n_kv * HEAD_DIM
    for pair in range(n_kv // 2):
        vt_ref[pair * LANES:(pair + 1) * LANES, :] = (
            kvbuf[:, dkv + pair * LANES:dkv + (pair + 1) * LANES].T)

    ones = jnp.ones((ONES_COLS, w), BF16)
    nt_dims = (((1,), (1,)), ((), ()))

    def scores(r0, kvh):
        c0 = kvh * GQ * HEAD_DIM
        qg = q_ref[pl.ds(r0, QBLK), c0:c0 + GQ * HEAD_DIM]
        qs = jnp.concatenate(
            [qg[:, g * HEAD_DIM:(g + 1) * HEAD_DIM] for g in range(GQ)], axis=0)
        k = kvbuf[pl.ds(r0, w), kvh * HEAD_DIM:(kvh + 1) * HEAD_DIM]
        return lax.dot_general(k, qs, nt_dims, preferred_element_type=F32)

    def softmax(st, r0, kvh, gs):
        seq_pos = (tile * ts + r0) & (seq_len - 1)
        prev_out = seq_pos == 0
        next_out = seq_pos + QBLK == seq_len
        ps, ms = [], []
        for g in gs:
            h = kvh * GQ + g
            top = jnp.where(prev_out, nh, h)
            bot = jnp.where(next_out, nh, h)
            bias = jnp.concatenate([bias_ref[top, 0:halo, :],
                                    bias_ref[h, halo:halo + QBLK, :],
                                    bias_ref[bot, halo + QBLK:w, :]], axis=0)
            sg = st[:, g * QBLK:(g + 1) * QBLK] + bias
            m = jnp.max(sg, axis=0, keepdims=True)
            if has_sink:
                m = jnp.maximum(m, sink_ref[sink_row, h] * LOG2E)
            ps.append(jnp.exp2(sg - m).astype(BF16))
            ms.append(m)
        return jnp.concatenate(ps, axis=1), ms

    def weighted_values(pt, ms, r0, kvh, gs):
        vt = vt_ref[kvh * HEAD_DIM:(kvh + 1) * HEAD_DIM, r0:r0 + w]
        vt_ext = jnp.concatenate([vt, ones], axis=0)
        ot = jnp.dot(vt_ext, pt, preferred_element_type=F32)
        outs, lses = [], []
        for n, g in enumerate(gs):
            h = kvh * GQ + g
            cs = slice(n * QBLK, (n + 1) * QBLK)
            l = ot[HEAD_DIM:HEAD_DIM + 1, cs]
            if has_sink:
                l = l + jnp.exp2(sink_ref[sink_row, h] * LOG2E - ms[n])
            outs.append(ot[0:HEAD_DIM, cs] * (1.0 / l))
            if emit_lse:
                lses.append(jnp.broadcast_to(ms[n] * LN2 + jnp.log(l), (HEAD_DIM, QBLK)))
        for n in range(0, len(gs), 2):
            c0 = (kvh * GQ + gs[n]) * HEAD_DIM
            o2 = jnp.concatenate(outs[n:n + 2], axis=0)
            o_ref[r0:r0 + QBLK, c0:c0 + LANES] = o2.T.astype(o_ref.dtype)
            if emit_lse:
                l2 = jnp.concatenate(lses[n:n + 2], axis=0)
                lse_ref[r0:r0 + QBLK, c0:c0 + LANES] = l2.T

    units = [(qb * QBLK, kvh) for qb in range(ts // QBLK) for kvh in range(n_kv)]
    head_sets = [tuple(range(i, i + HEADS_PER_PV)) for i in range(0, GQ, HEADS_PER_PV)]
    pending, probs = {}, {}

    def issue(u):
        pending[u] = scores(*units[u])

    def soft(u):
        st = pending.pop(u)
        probs[u] = [softmax(st, *units[u], gs) for gs in head_sets]

    def finish(u):
        for gs, (pt, ms) in zip(head_sets, probs.pop(u)):
            weighted_values(pt, ms, *units[u], gs)

    return [(functools.partial(issue, u), functools.partial(soft, u), functools.partial(finish, u))
            for u in range(len(units))]


def _attn_kernel(*refs, emit_lse, pv_delay, **cfg):
    refs = list(refs)
    q_ref, kv_refs = refs[0], refs[1:4]
    pos = 4
    sink_ref = None
    if cfg["sink_row"] is not None:
        sink_ref = refs[pos]
        pos += 1
    o_ref = refs[pos]
    pos += 1
    lse_ref = None
    if emit_lse:
        lse_ref = refs[pos]
        pos += 1
    kvbufs, bias_ref = refs[pos:pos + 2], refs[pos + 2]
    step = pl.program_id(0)
    units = _attn_units(q_ref, kv_refs, sink_ref, o_ref, lse_ref, kvbufs, bias_ref,
                        step, step == 0, **cfg)
    units[0][0]()
    for u, (_, soft, _) in enumerate(units):
        if u + 1 < len(units):
            units[u + 1][0]()
        soft()
        if u >= pv_delay:
            units[u - pv_delay][2]()
    for u in range(max(len(units) - pv_delay, 0), len(units)):
        units[u][2]()


def _attn_in_specs(s, ts, dq, dkv, halo):
    per = ts // halo
    last_halo = s // halo - 1
    return [
        pl.BlockSpec((ts, dq), lambda i: (i, 0)),
        pl.BlockSpec((halo, dkv), lambda i: (jnp.maximum(i * per - 1, 0), 0)),
        pl.BlockSpec((ts, dkv), lambda i: (i, 0)),
        pl.BlockSpec((halo, dkv), lambda i: (jnp.minimum((i + 1) * per, last_halo), 0)),
    ]


def _attn_scratch(ts, dkv, n_kv, halo):
    return [pltpu.VMEM((ts + 2 * halo, dkv), BF16),
            pltpu.VMEM((n_kv * HEAD_DIM, ts + 2 * halo), BF16),
            pltpu.VMEM((n_kv * GQ + 1, QBLK + 2 * halo, QBLK), F32)]


def _attention(q, kv, *, n_kv, halo, seq_len, dist_scale, slopes, pv_delay, units_per_step,
               sink=None, emit_lse=False, out_dtype=BF16):
    s, dq = q.shape
    dkv = kv.shape[1]
    ts = QBLK * (units_per_step // n_kv)
    assert s % ts == 0 and (ts % seq_len == 0 or seq_len % ts == 0)
    assert ts % halo == 0 and dq == n_kv * GQ * HEAD_DIM
    assert dkv == 2 * n_kv * HEAD_DIM
    in_specs = _attn_in_specs(s, ts, dq, dkv, halo)
    args = [q, kv, kv, kv]
    if sink is not None:
        in_specs.append(pl.BlockSpec(memory_space=pltpu.SMEM))
        args.append(sink.arr)
    out_shape = [jax.ShapeDtypeStruct((s, dq), out_dtype)]
    out_specs = [pl.BlockSpec((ts, dq), lambda i: (i, 0))]
    if emit_lse:
        out_shape.append(jax.ShapeDtypeStruct((s, dq), F32))
        out_specs.append(pl.BlockSpec((ts, dq), lambda i: (i, 0)))
    kern = functools.partial(
        _attn_kernel, emit_lse=emit_lse, pv_delay=pv_delay, n_kv=n_kv, halo=halo, seq_len=seq_len,
        dist_scale=dist_scale, slopes=tuple(float(x) for x in slopes),
        sink_row=None if sink is None else sink.idx)
    return pl.pallas_call(
        kern,
        out_shape=tuple(out_shape),
        grid=(s // ts,),
        in_specs=in_specs,
        out_specs=tuple(out_specs),
        scratch_shapes=_attn_scratch(ts, dkv, n_kv, halo),
        compiler_params=_params(),
        name="banded_attn",
    )(*args)


def _silu(x):
    return x * jax.nn.sigmoid(x)


def _post_tail(get_o, x_ref, mod_ref, g_ref, wo_ref, wi_ref, wo2_ref, fin_ref, out_ref):
    tm = x_ref.shape[0]
    rows = tm // POST_ROW_SLICES
    dff = wo2_ref.shape[0]
    chunks = ((0, FFN_SPLIT), (FFN_SPLIT, dff))

    def gate_up(st, c):
        lo, hi = chunks[c]
        st["gu"] = (jnp.dot(st["h2"], wi_ref[:, lo:hi], preferred_element_type=F32),
                    jnp.dot(st["h2"], wi_ref[:, dff + lo:dff + hi], preferred_element_type=F32))

    def down(st, c):
        lo, hi = chunks[c]
        gate, up = st["gu_prev"]
        act = (_silu(gate) * up).astype(BF16)
        part = jnp.dot(act, wo2_ref[lo:hi, :], preferred_element_type=F32)
        st["acc"] = part if c == 0 else st["acc"] + part

    def stage(st, k):
        if k == 0:
            st["y"] = jnp.dot(get_o(st["rs"]), wo_ref[...], preferred_element_type=F32)
        elif k == 1:
            st["x1"] = x_ref[st["rs"], :] + mod_ref[2:3, :] * st["y"]
            st["h2"] = _norm_mod(st["x1"], g_ref[...], mod_ref[3:4, :],
                                 mod_ref[4:5, :]).astype(BF16)
            gate_up(st, 0)
        elif k <= len(chunks):
            st["gu_prev"] = st["gu"]
            gate_up(st, k - 1)
            down(st, k - 2)
        else:
            st["gu_prev"] = st["gu"]
            down(st, len(chunks) - 1)
            x2 = st["x1"] + mod_ref[5:6, :] * st["acc"]
            if fin_ref is not None:
                ms = jnp.mean(x2 * x2, axis=-1, keepdims=True)
                x2 = (x2 * lax.rsqrt(ms + RMS_EPS)) * fin_ref[...]
            out_ref[st["rs"], :] = x2

    states = [{"rs": slice(n * rows, (n + 1) * rows)} for n in range(POST_ROW_SLICES)]
    for k in range(len(chunks) + 2):
        for st in states:
            stage(st, k)


def _merged_o(attn_refs, slabs, om_ref):
    o0_ref, l0_ref, o1_ref, l1_ref, o2_ref, l2_ref = attn_refs
    so1, sl1, so2, sl2 = slabs
    nc = B_DQ // LANES

    def merged(rs):
        for src_o, src_l, dst_o, dst_l in ((o1_ref, l1_ref, so1, sl1), (o2_ref, l2_ref, so2, sl2)):
            dil = src_o.shape[0]
            n_src = (rs.stop - rs.start) // dil
            src_rs = slice(rs.start // dil, rs.start // dil + n_src)
            for rr in range(dil):
                dst_rows = pl.ds(rs.start + rr, n_src, stride=dil)
                for c in range(nc):
                    cs = slice(c * LANES, (c + 1) * LANES)
                    dst_o[c, dst_rows, :] = src_o[rr, src_rs, cs]
                    dst_l[c, dst_rows, :] = src_l[rr, src_rs, cs]
        for c in range(nc):
            cs = slice(c * LANES, (c + 1) * LANES)
            l0, l1, l2 = l0_ref[0, rs, cs], sl1[c, rs, :], sl2[c, rs, :]
            mx = jnp.maximum(jnp.maximum(l0, l1), l2)
            w0, w1, w2 = jnp.exp(l0 - mx), jnp.exp(l1 - mx), jnp.exp(l2 - mx)
            num = w0 * o0_ref[0, rs, cs] + w1 * so1[c, rs, :] + w2 * so2[c, rs, :]
            om_ref[rs, cs] = (num / (w0 + w1 + w2)).astype(BF16)
        return om_ref[rs, :]

    return merged


def _post_kernel(*refs, n_attn, final, cast_cols, has_ada):
    refs = list(refs)
    attn_refs = refs[:n_attn]
    x_ref, mod_ref, g_ref, wo_ref, wi_ref, wo2_ref = refs[n_attn:n_attn + 6]
    pos = n_attn + 6
    fin_ref = None
    if final:
        fin_ref = refs[pos]
        pos += 1
    n_cast = len(cast_cols)
    cast_src = refs[pos:pos + n_cast]
    pos += n_cast
    ada_in = refs[pos:pos + 3] if has_ada else None
    pos += 3 if has_ada else 0
    out_ref = refs[pos]
    cast_dst = refs[pos + 1:pos + 1 + n_cast]
    pos += 1 + n_cast
    ada_out = refs[pos] if has_ada else None
    pos += 1 if has_ada else 0
    scratch = refs[pos:]

    _run_casts(cast_src, cast_dst, cast_cols)
    if has_ada:
        _ada_block(*ada_in, ada_out)

    if n_attn == 1:
        (o_ref,) = attn_refs
        get_o = lambda rs: o_ref[rs, :]
    else:
        get_o = _merged_o(attn_refs, scratch[:4], scratch[4])
    _post_tail(get_o, x_ref, mod_ref, g_ref, wo_ref, wi_ref, wo2_ref, fin_ref, out_ref)


def _post(o_parts, x, mod, wo, g, wi, wo2, final_g, *, casts=(), next_ada=None):
    s, d = x.shape
    tm = 512
    steps = s // tm
    final = final_g is not None
    dilated = len(o_parts) > 1
    in_specs, args, scratch = [], [], []
    if dilated:
        for arr in o_parts:
            dil = arr.shape[0]
            in_specs.append(pl.BlockSpec((dil, tm // dil, arr.shape[2]), lambda i: (0, i, 0)))
            args.append(arr)
        scratch = [pltpu.VMEM((B_DQ // LANES, tm, LANES), F32) for _ in range(4)]
        scratch.append(pltpu.VMEM((tm, B_DQ), BF16))
    else:
        (o,) = o_parts
        in_specs.append(pl.BlockSpec((tm, o.shape[1]), lambda i: (i, 0)))
        args.append(o)
    params = [mod, g, wo, wi, wo2] + ([final_g] if final else [])
    in_specs += [pl.BlockSpec((tm, d), lambda i: (i, 0))] + [_layer_spec(p) for p in params]
    args += [x] + [p.arr for p in params]
    out_shape = [jax.ShapeDtypeStruct((s, d), F32)]
    out_specs = [pl.BlockSpec((tm, d), lambda i: (i, 0))]
    cast_in, cast_args, cast_shape, cast_out = _cast_specs(casts, steps)
    in_specs += cast_in
    args += cast_args
    out_shape += cast_shape
    out_specs += cast_out
    if next_ada is not None:
        c_col, ada_w, ada_b3, layer = next_ada
        n = ada_w.shape[2]
        n_blk = next(k for k in range(steps, 0, -1) if n % k == 0 and (n // k) % LANES == 0)
        tn, last = n // n_blk, n_blk - 1
        ada_in, ada_out = _ada_specs(ada_w, layer, tn, lambda i: jnp.minimum(i, last))
        in_specs += ada_in
        args += [c_col, ada_w, ada_b3]
        out_shape.append(jax.ShapeDtypeStruct((1, n), F32))
        out_specs.append(ada_out)
    kern = functools.partial(
        _post_kernel, n_attn=len(o_parts), final=final,
        cast_cols=tuple(cst.cols for cst in casts), has_ada=next_ada is not None)
    return pl.pallas_call(
        kern,
        out_shape=tuple(out_shape),
        grid=(steps,),
        in_specs=in_specs,
        out_specs=tuple(out_specs),
        scratch_shapes=scratch,
        compiler_params=_params(),
        name="post_b" if dilated else "post_a",
    )(*args)


def _b_w_in_cols():
    ng = len(B_GROUPS)
    dq, dk = ng * B_DQ, ng * B_KV_HEADS * HEAD_DIM
    dkg = B_KV_HEADS * HEAD_DIM
    cols = []
    for gi in range(ng):
        cols += [(gi * B_DQ, B_DQ), (dq + gi * dkg, dkg), (dq + dk + gi * dkg, dkg)]
    return tuple(cols)


def kernel(x, c, ada_w, ada_b, norm_mix, norm_ffn, ffn_w_in, ffn_w_out,
           a_w_in, a_w_out, a_sink, b_w_in, b_w_out, final_norm):
    b, s, d = x.shape
    assert b == 1
    depth = ada_w.shape[0]
    xs = x.reshape(s, d)
    c_col = c.reshape(d, 1)
    ada_b3 = ada_b.reshape(depth, 1, -1)
    g_mix = norm_mix.reshape(depth, 1, d)
    g_ffn = norm_ffn.reshape(depth, 1, d)
    slopes_a = _alibi_slopes(A_Q_HEADS)
    slopes_b = _alibi_slopes(len(B_GROUPS) * B_Q_HEADS)
    b_cols = _b_w_in_cols()

    def mixer_params(i):
        return (a_w_in, a_w_out, i // 2, None) if i % 2 == 0 else (b_w_in, b_w_out, i // 2, b_cols)

    mod = _ada_mod(c_col, ada_w, ada_b3, 0)
    w_in = _Layer(a_w_in[0].astype(BF16)[None], 0)
    casts0 = (_Cast(a_w_out, 0), _Cast(ffn_w_in, 0), _Cast(ffn_w_out, 0))
    for i in range(depth):
        mod_l = _Layer(mod.reshape(1, 6, d), 0)
        if i % 2 == 0:
            q, kv, *cast0 = _qkv_a(xs, mod_l, _Layer(g_mix, i), w_in, casts0 if i == 0 else ())
            if i == 0:
                w_out, wi, wo2 = (_Layer(r[None], 0) for r in cast0)
            o_parts = _attention(q, kv, n_kv=A_KV_HEADS, halo=A_HALF_WINDOW, seq_len=s,
                                 dist_scale=1, slopes=slopes_a, pv_delay=A_PV_DELAY,
                                 units_per_step=A_UNITS_PER_STEP,
                                 sink=_Layer(a_sink, i // 2))
        else:
            outs = _qkv_b(xs, mod_l, _Layer(g_mix, i), w_in)
            o_parts = []
            for gi, (window, dil) in enumerate(B_GROUPS):
                qg = outs[2 * gi].reshape(s, B_DQ)
                kvg = outs[2 * gi + 1].reshape(s, B_DKV)
                og, lg = _attention(
                    qg, kvg, n_kv=B_KV_HEADS, halo=window // (2 * dil), seq_len=s // dil,
                    dist_scale=dil, slopes=slopes_b[gi * B_Q_HEADS:(gi + 1) * B_Q_HEADS],
                    pv_delay=B_PV_DELAY, units_per_step=B_UNITS_PER_STEP,
                    emit_lse=True, out_dtype=F32)
                o_parts += [og.reshape(dil, s // dil, B_DQ), lg.reshape(dil, s // dil, B_DQ)]
        last = i == depth - 1
        casts, next_ada = (), None
        if not last:
            m_in, m_out, j, cols = mixer_params(i + 1)
            casts = (_Cast(m_in, j, cols), _Cast(m_out, j),
                     _Cast(ffn_w_in, i + 1), _Cast(ffn_w_out, i + 1))
            next_ada = (c_col, ada_w, ada_b3, i + 1)
        fin = _Layer(final_norm.reshape(1, 1, d), 0) if last else None
        res = _post(o_parts, xs, mod_l, w_out, _Layer(g_ffn, i), wi, wo2, fin,
                    casts=casts, next_ada=next_ada)
        xs = res[0]
        if not last:
            w_in, w_out, wi, wo2 = (_Layer(r[None], 0) for r in res[1:5])
            mod = res[5]
    return xs.reshape(b, s, d)
```

```python
import functools
from typing import NamedTuple

import numpy as np
import jax
import jax.numpy as jnp
from jax import lax
from jax.experimental import pallas as pl
from jax.experimental.pallas import tpu as pltpu

HEAD_DIM = 64
A_Q_HEADS = 16
A_KV_HEADS = 4
A_HALF_WINDOW = 128
B_GROUPS = ((128, 1), (512, 4), (2048, 16))
B_Q_HEADS = 8
B_KV_HEADS = 2
GQ = 4
RMS_EPS = 1e-6
NEG = -1e30
LOG2E = 1.4426950408889634
LN2 = 0.6931471805599453
Q_SCALE = HEAD_DIM ** -0.5 * LOG2E

LANES = 128
QBLK = 128
A_UNITS_PER_STEP, B_UNITS_PER_STEP = 64, 64
ONES_COLS = 32
HEADS_PER_PV = 4
A_PV_DELAY, B_PV_DELAY = 0, 1
FFN_SPLIT = 1536
QKV_ROW_SLICES = 2
POST_ROW_SLICES = 2
VMEM_LIMIT = 56 * 1024 * 1024

BF16 = jnp.bfloat16
F32 = jnp.float32


def _alibi_slopes(n):
    return np.asarray(2.0 ** (-8.0 * np.arange(1, n + 1) / n), dtype=np.float32)


class _Layer(NamedTuple):
    arr: jax.Array
    idx: int


def _layer_spec(p):
    nd = p.arr.ndim
    return pl.BlockSpec((None,) + p.arr.shape[1:], lambda *_: (p.idx,) + (0,) * (nd - 1),
                        pipeline_mode=pl.Buffered(1))


def _params(n_axes=1, **kwargs):
    return pltpu.CompilerParams(
        dimension_semantics=("arbitrary",) * n_axes, vmem_limit_bytes=VMEM_LIMIT, **kwargs)


def _ada_block(c_ref, w_ref, b_ref, o_ref):
    c = c_ref[...]
    cond = c * jax.nn.sigmoid(c)
    o_ref[...] = jnp.sum(w_ref[...] * cond, axis=0, keepdims=True) + b_ref[...]


def _ada_specs(ada_w, layer, tn, col_block):
    d = ada_w.shape[1]
    return ([pl.BlockSpec((d, 1), lambda *i: (0, 0)),
             pl.BlockSpec((None, d, tn), lambda *i: (layer, 0, col_block(*i))),
             pl.BlockSpec((None, 1, tn), lambda *i: (layer, 0, col_block(*i)))],
            pl.BlockSpec((1, tn), lambda *i: (0, col_block(*i))))


def _ada_mod(c_col, ada_w, ada_b3, layer):
    n = ada_w.shape[2]
    tn = 1536
    in_specs, out_spec = _ada_specs(ada_w, layer, tn, lambda j: j)
    return pl.pallas_call(
        _ada_block,
        out_shape=jax.ShapeDtypeStruct((1, n), F32),
        grid=(n // tn,),
        in_specs=in_specs,
        out_specs=out_spec,
        compiler_params=_params(),
        name="ada_mod",
    )(c_col, ada_w, ada_b3)


def _norm_mod(x, g, shift, scale):
    ms = jnp.mean(x * x, axis=-1, keepdims=True)
    y = x * lax.rsqrt(ms + RMS_EPS)
    return y * (g * (1.0 + scale)) + shift


class _Cast(NamedTuple):
    arr: jax.Array
    idx: int
    cols: tuple | None = None


BF16_SUBLANES = 16


def _cast_specs(casts, steps):
    in_specs, args, out_shape, out_specs = [], [], [], []
    for cst in casts:
        _, n_rows, n_cols = cst.arr.shape
        n_blk = next(n for n in range(steps, 0, -1)
                     if n_rows % n == 0 and (n_rows // n) % BF16_SUBLANES == 0)
        rows, last = n_rows // n_blk, n_blk - 1
        in_specs.append(pl.BlockSpec((None, rows, n_cols),
                                     lambda i, cst=cst, last=last: (cst.idx, jnp.minimum(i, last), 0)))
        args.append(cst.arr)
        out_shape.append(jax.ShapeDtypeStruct((n_rows, n_cols), BF16))
        out_specs.append(pl.BlockSpec((rows, n_cols),
                                      lambda i, last=last: (jnp.minimum(i, last), 0)))
    return in_specs, args, out_shape, out_specs


def _run_casts(srcs, dsts, cast_cols):
    for src, dst, cols in zip(srcs, dsts, cast_cols):
        if cols is None:
            dst[...] = src[...].astype(BF16)
        else:
            at = 0
            for start, width in cols:
                dst[:, at:at + width] = src[:, start:start + width].astype(BF16)
                at += width


def _qkv_a_kernel(x_ref, mod_ref, g_ref, w_ref, *refs, cast_cols):
    n_cast = len(cast_cols)
    cast_src, (q_ref, kv_ref), cast_dst = refs[:n_cast], refs[n_cast:n_cast + 2], refs[n_cast + 2:]
    _run_casts(cast_src, cast_dst, cast_cols)
    dq = q_ref.shape[1]
    rows = x_ref.shape[0] // QKV_ROW_SLICES
    for n in range(QKV_ROW_SLICES):
        rs = slice(n * rows, (n + 1) * rows)
        h = _norm_mod(x_ref[rs, :], g_ref[...], mod_ref[0:1, :], mod_ref[1:2, :])
        r = jnp.dot(h.astype(BF16), w_ref[...], preferred_element_type=F32)
        q_ref[rs, :] = (r[:, :dq] * Q_SCALE).astype(BF16)
        kv_ref[rs, :] = r[:, dq:].astype(BF16)


def _qkv_a(x, mod, g, w, casts=()):
    s, d = x.shape
    dq = A_Q_HEADS * HEAD_DIM
    dkv = w.arr.shape[2] - dq
    tm = 1024
    cast_in, cast_args, cast_shape, cast_out = _cast_specs(casts, s // tm)
    return pl.pallas_call(
        functools.partial(_qkv_a_kernel, cast_cols=tuple(cst.cols for cst in casts)),
        out_shape=(jax.ShapeDtypeStruct((s, dq), BF16),
                   jax.ShapeDtypeStruct((s, dkv), BF16), *cast_shape),
        grid=(s // tm,),
        in_specs=[
            pl.BlockSpec((tm, d), lambda i: (i, 0)),
            _layer_spec(mod),
            _layer_spec(g),
            _layer_spec(w),
            *cast_in,
        ],
        out_specs=(pl.BlockSpec((tm, dq), lambda i: (i, 0)),
                   pl.BlockSpec((tm, dkv), lambda i: (i, 0)), *cast_out),
        compiler_params=_params(),
        name="qkv_a",
    )(x, mod.arr, g.arr, w.arr, *cast_args)


B_DQ = B_Q_HEADS * HEAD_DIM
B_DKV = 2 * B_KV_HEADS * HEAD_DIM
B_DG = B_DQ + B_DKV


def _qkv_b_kernel(x_ref, mod_ref, g_ref, w_ref, *refs):
    out_refs, slab_ref = refs[:-1], refs[-1]
    tm = x_ref.shape[0]
    h = _norm_mod(x_ref[...], g_ref[...], mod_ref[0:1, :], mod_ref[1:2, :])
    hb = h.astype(BF16)
    nq = B_DQ // LANES
    for gi, (_, dil) in enumerate(B_GROUPS):
        q_ref, kv_ref = out_refs[2 * gi], out_refs[2 * gi + 1]
        r = jnp.dot(hb, w_ref[:, gi * B_DG:(gi + 1) * B_DG], preferred_element_type=F32)
        rq = r[:, :B_DQ] * Q_SCALE
        rkv = r[:, B_DQ:]
        if dil == 1:
            q_ref[0] = rq.astype(BF16)
            kv_ref[0] = rkv.astype(BF16)
            continue
        for c in range(B_DG // LANES):
            src = rq if c < nq else rkv
            cc = c if c < nq else c - nq
            slab_ref[c] = src[:, cc * LANES:(cc + 1) * LANES]
        rows = tm // dil
        for rr in range(dil):
            for c in range(B_DG // LANES):
                v = slab_ref[c, pl.ds(rr, rows, stride=dil), :].astype(BF16)
                if c < nq:
                    q_ref[rr, :, c * LANES:(c + 1) * LANES] = v
                else:
                    kv_ref[rr, :, (c - nq) * LANES:(c - nq + 1) * LANES] = v


def _qkv_b(x, mod, g, w):
    s, d = x.shape
    tm = 1024
    out_shape, out_specs = [], []
    for _, dil in B_GROUPS:
        for width in (B_DQ, B_DKV):
            out_shape.append(jax.ShapeDtypeStruct((dil, s // dil, width), BF16))
            out_specs.append(pl.BlockSpec((dil, tm // dil, width), lambda i: (0, i, 0)))
    return pl.pallas_call(
        _qkv_b_kernel,
        out_shape=tuple(out_shape),
        grid=(s // tm,),
        in_specs=[
            pl.BlockSpec((tm, d), lambda i: (i, 0)),
            _layer_spec(mod),
            _layer_spec(g),
            _layer_spec(w),
        ],
        out_specs=tuple(out_specs),
        scratch_shapes=[pltpu.VMEM((B_DG // LANES, tm, LANES), F32)],
        compiler_params=_params(),
        name="qkv_b",
    )(x, mod.arr, g.arr, w.arr)


def _attn_units(q_ref, kv_refs, sink_ref, o_ref, lse_ref, bufs, bias_ref, tile, first_step,
                *, n_kv, halo, seq_len, dist_scale, slopes, sink_row):
    kvp_ref, kvc_ref, kvn_ref = kv_refs
    has_sink = sink_row is not None
    emit_lse = lse_ref is not None
    ts = q_ref.shape[0]
    w = QBLK + 2 * halo
    nh = n_kv * GQ

    @pl.when(first_step)
    def _():
        kj = lax.broadcasted_iota(jnp.int32, (w, QBLK), 0)
        qi = lax.broadcasted_iota(jnp.int32, (w, QBLK), 1)
        rel = jnp.abs(kj - halo - qi)
        dist = (dist_scale * rel).astype(F32)
        for h, slope in enumerate(slopes):
            bias_ref[h] = jnp.where(rel <= halo, ((-float(slope)) * dist) * LOG2E, NEG)
        bias_ref[nh] = jnp.full((w, QBLK), NEG, F32)

    kvbuf, vt_ref = bufs
    kvbuf[0:halo, :] = kvp_ref[...]
    kvbuf[halo:halo + ts, :] = kvc_ref[...]
    kvbuf[halo + ts:, :] = kvn_ref[...]
    dkv = n_kv * HEAD_DIM
    for pair in range(n_kv // 2):
        vt_ref[pair * LANES:(pair + 1) * LANES, :] = (
            kvbuf[:, dkv + pair * LANES:dkv + (pair + 1) * LANES].T)

    ones = jnp.ones((ONES_COLS, w), BF16)
    nt_dims = (((1,), (1,)), ((), ()))

    def scores(r0, kvh):
        c0 = kvh * GQ * HEAD_DIM
        qg = q_ref[pl.ds(r0, QBLK), c0:c0 + GQ * HEAD_DIM]
        qs = jnp.concatenate(
            [qg[:, g * HEAD_DIM:(g + 1) * HEAD_DIM] for g in range(GQ)], axis=0)
        k = kvbuf[pl.ds(r0, w), kvh * HEAD_DIM:(kvh + 1) * HEAD_DIM]
        return lax.dot_general(k, qs, nt_dims, preferred_element_type=F32)

    def softmax(st, r0, kvh, gs):
        seq_pos = (tile * ts + r0) & (seq_len - 1)
        prev_out = seq_pos == 0
        next_out = seq_pos + QBLK == seq_len
        ps, ms = [], []
        for g in gs:
            h = kvh * GQ + g
            top = jnp.where(prev_out, nh, h)
            bot = jnp.where(next_out, nh, h)
            bias = jnp.concatenate([bias_ref[top, 0:halo, :],
                                    bias_ref[h, halo:halo + QBLK, :],
                                    bias_ref[bot, halo + QBLK:w, :]], axis=0)
            sg = st[:, g * QBLK:(g + 1) * QBLK] + bias
            m = jnp.max(sg, axis=0, keepdims=True)
            if has_sink:
                m = jnp.maximum(m, sink_ref[sink_row, h] * LOG2E)
            ps.append(jnp.exp2(sg - m).astype(BF16))
            ms.append(m)
        return jnp.concatenate(ps, axis=1), ms

    def weighted_values(pt, ms, r0, kvh, gs):
        vt = vt_ref[kvh * HEAD_DIM:(kvh + 1) * HEAD_DIM, r0:r0 + w]
        vt_ext = jnp.concatenate([vt, ones], axis=0)
        ot = jnp.dot(vt_ext, pt, preferred_element_type=F32)
        outs, lses = [], []
        for n, g in enumerate(gs):
            h = kvh * GQ + g
            cs = slice(n * QBLK, (n + 1) * QBLK)
            l = ot[HEAD_DIM:HEAD_DIM + 1, cs]
            if has_sink:
                l = l + jnp.exp2(sink_ref[sink_row, h] * LOG2E - ms[n])
            outs.append(ot[0:HEAD_DIM, cs] * (1.0 / l))
            if emit_lse:
                lses.append(jnp.broadcast_to(ms[n] * LN2 + jnp.log(l), (HEAD_DIM, QBLK)))
        for n in range(0, len(gs), 2):
            c0 = (kvh * GQ + gs[n]) * HEAD_DIM
            o2 = jnp.concatenate(outs[n:n + 2], axis=0)
            o_ref[r0:r0 + QBLK, c0:c0 + LANES] = o2.T.astype(o_ref.dtype)
            if emit_lse:
                l2 = jnp.concatenate(lses[n:n + 2], axis=0)
                lse_ref[r0:r0 + QBLK, c0:c0 + LANES] = l2.T

    units = [(qb * QBLK, kvh) for qb in range(ts // QBLK) for kvh in range(n_kv)]
    head_sets = [tuple(range(i, i + HEADS_PER_PV)) for i in range(0, GQ, HEADS_PER_PV)]
    pending, probs = {}, {}

    def issue(u):
        pending[u] = scores(*units[u])

    def soft(u):
        st = pending.pop(u)
        probs[u] = [softmax(st, *units[u], gs) for gs in head_sets]

    def finish(u):
        for gs, (pt, ms) in zip(head_sets, probs.pop(u)):
            weighted_values(pt, ms, *units[u], gs)

    return [(functools.partial(issue, u), functools.partial(soft, u), functools.partial(finish, u))
            for u in range(len(units))]


def _attn_kernel(*refs, emit_lse, pv_delay, **cfg):
    refs = list(refs)
    q_ref, kv_refs = refs[0], refs[1:4]
    pos = 4
    sink_ref = None
    if cfg["sink_row"] is not None:
        sink_ref = refs[pos]
        pos += 1
    o_ref = refs[pos]
    pos += 1
    lse_ref = None
    if emit_lse:
        lse_ref = refs[pos]
        pos += 1
    kvbufs, bias_ref = refs[pos:pos + 2], refs[pos + 2]
    step = pl.program_id(0)
    units = _attn_units(q_ref, kv_refs, sink_ref, o_ref, lse_ref, kvbufs, bias_ref,
                        step, step == 0, **cfg)
    units[0][0]()
    for u, (_, soft, _) in enumerate(units):
        if u + 1 < len(units):
            units[u + 1][0]()
        soft()
        if u >= pv_delay:
            units[u - pv_delay][2]()
    for u in range(max(len(units) - pv_delay, 0), len(units)):
        units[u][2]()


def _attn_in_specs(s, ts, dq, dkv, halo):
    per = ts // halo
    last_halo = s // halo - 1
    return [
        pl.BlockSpec((ts, dq), lambda i: (i, 0)),
        pl.BlockSpec((halo, dkv), lambda i: (jnp.maximum(i * per - 1, 0), 0)),
        pl.BlockSpec((ts, dkv), lambda i: (i, 0)),
        pl.BlockSpec((halo, dkv), lambda i: (jnp.minimum((i + 1) * per, last_halo), 0)),
    ]


def _attn_scratch(ts, dkv, n_kv, halo):
    return [pltpu.VMEM((ts + 2 * halo, dkv), BF16),
            pltpu.VMEM((n_kv * HEAD_DIM, ts + 2 * halo), BF16),
            pltpu.VMEM((n_kv * GQ + 1, QBLK + 2 * halo, QBLK), F32)]


def _attention(q, kv, *, n_kv, halo, seq_len, dist_scale, slopes, pv_delay, units_per_step,
               sink=None, emit_lse=False, out_dtype=BF16):
    s, dq = q.shape
    dkv = kv.shape[1]
    ts = QBLK * (units_per_step // n_kv)
    assert s % ts == 0 and (ts % seq_len == 0 or seq_len % ts == 0)
    assert ts % halo == 0 and dq == n_kv * GQ * HEAD_DIM
    assert dkv == 2 * n_kv * HEAD_DIM
    in_specs = _attn_in_specs(s, ts, dq, dkv, halo)
    args = [q, kv, kv, kv]
    if sink is not None:
        in_specs.append(pl.BlockSpec(memory_space=pltpu.SMEM))
        args.append(sink.arr)
    out_shape = [jax.ShapeDtypeStruct((s, dq), out_dtype)]
    out_specs = [pl.BlockSpec((ts, dq), lambda i: (i, 0))]
    if emit_lse:
        out_shape.append(jax.ShapeDtypeStruct((s, dq), F32))
        out_specs.append(pl.BlockSpec((ts, dq), lambda i: (i, 0)))
    kern = functools.partial(
        _attn_kernel, emit_lse=emit_lse, pv_delay=pv_delay, n_kv=n_kv, halo=halo, seq_len=seq_len,
        dist_scale=dist_scale, slopes=tuple(float(x) for x in slopes),
        sink_row=None if sink is None else sink.idx)
    return pl.pallas_call(
        kern,
        out_shape=tuple(out_shape),
        grid=(s // ts,),
        in_specs=in_specs,
        out_specs=tuple(out_specs),
        scratch_shapes=_attn_scratch(ts, dkv, n_kv, halo),
        compiler_params=_params(),
        name="banded_attn",
    )(*args)


def _silu(x):
    return x * jax.nn.sigmoid(x)


def _post_tail(get_o, x_ref, mod_ref, g_ref, wo_ref, wi_ref, wo2_ref, fin_ref, out_ref):
    tm = x_ref.shape[0]
    rows = tm // POST_ROW_SLICES
    dff = wo2_ref.shape[0]
    chunks = ((0, FFN_SPLIT), (FFN_SPLIT, dff))

    def gate_up(st, c):
        lo, hi = chunks[c]
        st["gu"] = (jnp.dot(st["h2"], wi_ref[:, lo:hi], preferred_element_type=F32),
                    jnp.dot(st["h2"], wi_ref[:, dff + lo:dff + hi], preferred_element_type=F32))

    def down(st, c):
        lo, hi = chunks[c]
        gate, up = st["gu_prev"]
        act = (_silu(gate) * up).astype(BF16)
        part = jnp.dot(act, wo2_ref[lo:hi, :], preferred_element_type=F32)
        st["acc"] = part if c == 0 else st["acc"] + part

    def stage(st, k):
        if k == 0:
            st["y"] = jnp.dot(get_o(st["rs"]), wo_ref[...], preferred_element_type=F32)
        elif k == 1:
            st["x1"] = x_ref[st["rs"], :] + mod_ref[2:3, :] * st["y"]
            st["h2"] = _norm_mod(st["x1"], g_ref[...], mod_ref[3:4, :],
                                 mod_ref[4:5, :]).astype(BF16)
            gate_up(st, 0)
        elif k <= len(chunks):
            st["gu_prev"] = st["gu"]
            gate_up(st, k - 1)
            down(st, k - 2)
        else:
            st["gu_prev"] = st["gu"]
            down(st, len(chunks) - 1)
            x2 = st["x1"] + mod_ref[5:6, :] * st["acc"]
            if fin_ref is not None:
                ms = jnp.mean(x2 * x2, axis=-1, keepdims=True)
                x2 = (x2 * lax.rsqrt(ms + RMS_EPS)) * fin_ref[...]
            out_ref[st["rs"], :] = x2

    states = [{"rs": slice(n * rows, (n + 1) * rows)} for n in range(POST_ROW_SLICES)]
    for k in range(len(chunks) + 2):
        for st in states:
            stage(st, k)


def _merged_o(attn_refs, slabs, om_ref):
    o0_ref, l0_ref, o1_ref, l1_ref, o2_ref, l2_ref = attn_refs
    so1, sl1, so2, sl2 = slabs
    nc = B_DQ // LANES

    def merged(rs):
        for src_o, src_l, dst_o, dst_l in ((o1_ref, l1_ref, so1, sl1), (o2_ref, l2_ref, so2, sl2)):
            dil = src_o.shape[0]
            n_src = (rs.stop - rs.start) // dil
            src_rs = slice(rs.start // dil, rs.start // dil + n_src)
            for rr in range(dil):
                dst_rows = pl.ds(rs.start + rr, n_src, stride=dil)
                for c in range(nc):
                    cs = slice(c * LANES, (c + 1) * LANES)
                    dst_o[c, dst_rows, :] = src_o[rr, src_rs, cs]
                    dst_l[c, dst_rows, :] = src_l[rr, src_rs, cs]
        for c in range(nc):
            cs = slice(c * LANES, (c + 1) * LANES)
            l0, l1, l2 = l0_ref[0, rs, cs], sl1[c, rs, :], sl2[c, rs, :]
            mx = jnp.maximum(jnp.maximum(l0, l1), l2)
            w0, w1, w2 = jnp.exp(l0 - mx), jnp.exp(l1 - mx), jnp.exp(l2 - mx)
            num = w0 * o0_ref[0, rs, cs] + w1 * so1[c, rs, :] + w2 * so2[c, rs, :]
            om_ref[rs, cs] = (num / (w0 + w1 + w2)).astype(BF16)
        return om_ref[rs, :]

    return merged


def _post_kernel(*refs, n_attn, final, cast_cols, has_ada):
    refs = list(refs)
    attn_refs = refs[:n_attn]
    x_ref, mod_ref, g_ref, wo_ref, wi_ref, wo2_ref = refs[n_attn:n_attn + 6]
    pos = n_attn + 6
    fin_ref = None
    if final:
        fin_ref = refs[pos]
        pos += 1
    n_cast = len(cast_cols)
    cast_src = refs[pos:pos + n_cast]
    pos += n_cast
    ada_in = refs[pos:pos + 3] if has_ada else None
    pos += 3 if has_ada else 0
    out_ref = refs[pos]
    cast_dst = refs[pos + 1:pos + 1 + n_cast]
    pos += 1 + n_cast
    ada_out = refs[pos] if has_ada else None
    pos += 1 if has_ada else 0
    scratch = refs[pos:]

    _run_casts(cast_src, cast_dst, cast_cols)
    if has_ada:
        _ada_block(*ada_in, ada_out)

    if n_attn == 1:
        (o_ref,) = attn_refs
        get_o = lambda rs: o_ref[rs, :]
    else:
        get_o = _merged_o(attn_refs, scratch[:4], scratch[4])
    _post_tail(get_o, x_ref, mod_ref, g_ref, wo_ref, wi_ref, wo2_ref, fin_ref, out_ref)


def _post(o_parts, x, mod, wo, g, wi, wo2, final_g, *, casts=(), next_ada=None):
    s, d = x.shape
    tm = 512
    steps = s // tm
    final = final_g is not None
    dilated = len(o_parts) > 1
    in_specs, args, scratch = [], [], []
    if dilated:
        for arr in o_parts:
            dil = arr.shape[0]
            in_specs.append(pl.BlockSpec((dil, tm // dil, arr.shape[2]), lambda i: (0, i, 0)))
            args.append(arr)
        scratch = [pltpu.VMEM((B_DQ // LANES, tm, LANES), F32) for _ in range(4)]
        scratch.append(pltpu.VMEM((tm, B_DQ), BF16))
    else:
        (o,) = o_parts
        in_specs.append(pl.BlockSpec((tm, o.shape[1]), lambda i: (i, 0)))
        args.append(o)
    params = [mod, g, wo, wi, wo2] + ([final_g] if final else [])
    in_specs += [pl.BlockSpec((tm, d), lambda i: (i, 0))] + [_layer_spec(p) for p in params]
    args += [x] + [p.arr for p in params]
    out_shape = [jax.ShapeDtypeStruct((s, d), F32)]
    out_specs = [pl.BlockSpec((tm, d), lambda i: (i, 0))]
    cast_in, cast_args, cast_shape, cast_out = _cast_specs(casts, steps)
    in_specs += cast_in
    args += cast_args
    out_shape += cast_shape
    out_specs += cast_out
    if next_ada is not None:
        c_col, ada_w, ada_b3, layer = next_ada
        n = ada_w.shape[2]
        n_blk = next(k for k in range(steps, 0, -1) if n % k == 0 and (n // k) % LANES == 0)
        tn, last = n // n_blk, n_blk - 1
        ada_in, ada_out = _ada_specs(ada_w, layer, tn, lambda i: jnp.minimum(i, last))
        in_specs += ada_in
        args += [c_col, ada_w, ada_b3]
        out_shape.append(jax.ShapeDtypeStruct((1, n), F32))
        out_specs.append(ada_out)
    kern = functools.partial(
        _post_kernel, n_attn=len(o_parts), final=final,
        cast_cols=tuple(cst.cols for cst in casts), has_ada=next_ada is not None)
    return pl.pallas_call(
        kern,
        out_shape=tuple(out_shape),
        grid=(steps,),
        in_specs=in_specs,
        out_specs=tuple(out_specs),
        scratch_shapes=scratch,
        compiler_params=_params(),
        name="post_b" if dilated else "post_a",
    )(*args)


def _b_w_in_cols():
    ng = len(B_GROUPS)
    dq, dk = ng * B_DQ, ng * B_KV_HEADS * HEAD_DIM
    dkg = B_KV_HEADS * HEAD_DIM
    cols = []
    for gi in range(ng):
        cols += [(gi * B_DQ, B_DQ), (dq + gi * dkg, dkg), (dq + dk + gi * dkg, dkg)]
    return tuple(cols)


def kernel(x, c, ada_w, ada_b, norm_mix, norm_ffn, ffn_w_in, ffn_w_out,
           a_w_in, a_w_out, a_sink, b_w_in, b_w_out, final_norm):
    b, s, d = x.shape
    assert b == 1
    depth = ada_w.shape[0]
    xs = x.reshape(s, d)
    c_col = c.reshape(d, 1)
    ada_b3 = ada_b.reshape(depth, 1, -1)
    g_mix = norm_mix.reshape(depth, 1, d)
    g_ffn = norm_ffn.reshape(depth, 1, d)
    slopes_a = _alibi_slopes(A_Q_HEADS)
    slopes_b = _alibi_slopes(len(B_GROUPS) * B_Q_HEADS)
    b_cols = _b_w_in_cols()

    def mixer_params(i):
        return (a_w_in, a_w_out, i // 2, None) if i % 2 == 0 else (b_w_in, b_w_out, i // 2, b_cols)

    mod = _ada_mod(c_col, ada_w, ada_b3, 0)
    w_in = _Layer(a_w_in[0].astype(BF16)[None], 0)
    casts0 = (_Cast(a_w_out, 0), _Cast(ffn_w_in, 0), _Cast(ffn_w_out, 0))
    for i in range(depth):
        mod_l = _Layer(mod.reshape(1, 6, d), 0)
        if i % 2 == 0:
            q, kv, *cast0 = _qkv_a(xs, mod_l, _Layer(g_mix, i), w_in, casts0 if i == 0 else ())
            if i == 0:
                w_out, wi, wo2 = (_Layer(r[None], 0) for r in cast0)
            o_parts = _attention(q, kv, n_kv=A_KV_HEADS, halo=A_HALF_WINDOW, seq_len=s,
                                 dist_scale=1, slopes=slopes_a, pv_delay=A_PV_DELAY,
                                 units_per_step=A_UNITS_PER_STEP,
                                 sink=_Layer(a_sink, i // 2))
        else:
            outs = _qkv_b(xs, mod_l, _Layer(g_mix, i), w_in)
            o_parts = []
            for gi, (window, dil) in enumerate(B_GROUPS):
                qg = outs[2 * gi].reshape(s, B_DQ)
                kvg = outs[2 * gi + 1].reshape(s, B_DKV)
                og, lg = _attention(
                    qg, kvg, n_kv=B_KV_HEADS, halo=window // (2 * dil), seq_len=s // dil,
                    dist_scale=dil, slopes=slopes_b[gi * B_Q_HEADS:(gi + 1) * B_Q_HEADS],
                    pv_delay=B_PV_DELAY, units_per_step=B_UNITS_PER_STEP,
                    emit_lse=True, out_dtype=F32)
                o_parts += [og.reshape(dil, s // dil, B_DQ), lg.reshape(dil, s // dil, B_DQ)]
        last = i == depth - 1
        casts, next_ada = (), None
        if not last:
            m_in, m_out, j, cols = mixer_params(i + 1)
            casts = (_Cast(m_in, j, cols), _Cast(m_out, j),
                     _Cast(ffn_w_in, i + 1), _Cast(ffn_w_out, i + 1))
            next_ada = (c_col, ada_w, ada_b3, i + 1)
        fin = _Layer(final_norm.reshape(1, 1, d), 0) if last else None
        res = _post(o_parts, xs, mod_l, w_out, _Layer(g_ffn, i), wi, wo2, fin,
                    casts=casts, next_ada=next_ada)
        xs = res[0]
        if not last:
            w_in, w_out, wi, wo2 = (_Layer(r[None], 0) for r in res[1:5])
            mod = res[5]
    return xs.reshape(b, s, d)
```

```python
import functools
from typing import NamedTuple

import numpy as np
import jax
import jax.numpy as jnp
from jax import lax
from jax.experimental import pallas as pl
from jax.experimental.pallas import tpu as pltpu

HEAD_DIM = 64
A_Q_HEADS = 16
A_KV_HEADS = 4
A_HALF_WINDOW = 128
B_GROUPS = ((128, 1), (512, 4), (2048, 16))
B_Q_HEADS = 8
B_KV_HEADS = 2
GQ = 4
RMS_EPS = 1e-6
NEG = -1e30
LOG2E = 1.4426950408889634
LN2 = 0.6931471805599453
Q_SCALE = HEAD_DIM ** -0.5 * LOG2E

LANES = 128
QBLK = 128
A_UNITS_PER_STEP, B_UNITS_PER_STEP = 64, 64
ONES_COLS = 32
HEADS_PER_PV = 4
A_PV_DELAY, B_PV_DELAY = 0, 1
FFN_SPLIT = 1536
QKV_ROW_SLICES = 2
POST_ROW_SLICES = 2
VMEM_LIMIT = 56 * 1024 * 1024

BF16 = jnp.bfloat16
F32 = jnp.float32


def _alibi_slopes(n):
    return np.asarray(2.0 ** (-8.0 * np.arange(1, n + 1) / n), dtype=np.float32)


class _Layer(NamedTuple):
    arr: jax.Array
    idx: int


def _layer_spec(p):
    nd = p.arr.ndim
    return pl.BlockSpec((None,) + p.arr.shape[1:], lambda *_: (p.idx,) + (0,) * (nd - 1),
                        pipeline_mode=pl.Buffered(1))


def _params(n_axes=1, **kwargs):
    return pltpu.CompilerParams(
        dimension_semantics=("arbitrary",) * n_axes, vmem_limit_bytes=VMEM_LIMIT, **kwargs)


def _ada_block(c_ref, w_ref, b_ref, o_ref):
    c = c_ref[...]
    cond = c * jax.nn.sigmoid(c)
    o_ref[...] = jnp.sum(w_ref[...] * cond, axis=0, keepdims=True) + b_ref[...]


def _ada_specs(ada_w, layer, tn, col_block):
    d, n = ada_w.shape[1:]
    per = d // tn
    assert d % tn == 0 and n % d == 0
    return ([pl.BlockSpec((d, 1), lambda *i: (0, 0)),
             pl.BlockSpec((None, d, tn), lambda *i: (layer, 0, col_block(*i))),
             pl.BlockSpec((None, 1, tn), lambda *i: (layer, 0, col_block(*i)))],
            pl.BlockSpec((None, 1, tn), lambda *i: (col_block(*i) // per, 0, col_block(*i) % per)),
            jax.ShapeDtypeStruct((n // d, 1, d), F32))


def _ada_mod(c_col, ada_w, ada_b3, layer):
    d, n = ada_w.shape[1:]
    in_specs, out_spec, out_shape = _ada_specs(ada_w, layer, d, lambda j: j)
    return pl.pallas_call(
        _ada_block,
        out_shape=out_shape,
        grid=(n // d,),
        in_specs=in_specs,
        out_specs=out_spec,
        compiler_params=_params(),
        name="ada_mod",
    )(c_col, ada_w, ada_b3)


def _norm_mod(x, g, shift, scale):
    ms = jnp.mean(x * x, axis=-1, keepdims=True)
    y = x * lax.rsqrt(ms + RMS_EPS)
    return y * (g * (1.0 + scale)) + shift


class _Cast(NamedTuple):
    arr: jax.Array
    idx: int
    cols: tuple | None = None


BF16_SUBLANES = 16


def _cast_specs(casts, steps):
    in_specs, args, out_shape, out_specs = [], [], [], []
    for cst in casts:
        _, n_rows, n_cols = cst.arr.shape
        n_blk = next(n for n in range(steps, 0, -1)
                     if n_rows % n == 0 and (n_rows // n) % BF16_SUBLANES == 0)
        rows, last = n_rows // n_blk, n_blk - 1
        in_specs.append(pl.BlockSpec((None, rows, n_cols),
                                     lambda i, cst=cst, last=last: (cst.idx, jnp.minimum(i, last), 0)))
        args.append(cst.arr)
        out_shape.append(jax.ShapeDtypeStruct((n_rows, n_cols), BF16))
        out_specs.append(pl.BlockSpec((rows, n_cols),
                                      lambda i, last=last: (jnp.minimum(i, last), 0)))
    return in_specs, args, out_shape, out_specs


def _run_casts(srcs, dsts, cast_cols):
    for src, dst, cols in zip(srcs, dsts, cast_cols):
        if cols is None:
            dst[...] = src[...].astype(BF16)
        else:
            at = 0
            for start, width in cols:
                dst[:, at:at + width] = src[:, start:start + width].astype(BF16)
                at += width


def _qkv_a_kernel(x_ref, mod_ref, g_ref, w_ref, *refs, cast_cols):
    n_cast = len(cast_cols)
    cast_src, (q_ref, kv_ref), cast_dst = refs[:n_cast], refs[n_cast:n_cast + 2], refs[n_cast + 2:]
    _run_casts(cast_src, cast_dst, cast_cols)
    dq = q_ref.shape[1]
    rows = x_ref.shape[0] // QKV_ROW_SLICES
    for n in range(QKV_ROW_SLICES):
        rs = slice(n * rows, (n + 1) * rows)
        h = _norm_mod(x_ref[rs, :], g_ref[...], mod_ref[0], mod_ref[1])
        r = jnp.dot(h.astype(BF16), w_ref[...], preferred_element_type=F32)
        q_ref[rs, :] = (r[:, :dq] * Q_SCALE).astype(BF16)
        kv_ref[rs, :] = r[:, dq:].astype(BF16)


def _qkv_a(x, mod, g, w, casts=()):
    s, d = x.shape
    dq = A_Q_HEADS * HEAD_DIM
    dkv = w.arr.shape[2] - dq
    tm = 1024
    cast_in, cast_args, cast_shape, cast_out = _cast_specs(casts, s // tm)
    return pl.pallas_call(
        functools.partial(_qkv_a_kernel, cast_cols=tuple(cst.cols for cst in casts)),
        out_shape=(jax.ShapeDtypeStruct((s, dq), BF16),
                   jax.ShapeDtypeStruct((s, dkv), BF16), *cast_shape),
        grid=(s // tm,),
        in_specs=[
            pl.BlockSpec((tm, d), lambda i: (i, 0)),
            _layer_spec(mod),
            _layer_spec(g),
            _layer_spec(w),
            *cast_in,
        ],
        out_specs=(pl.BlockSpec((tm, dq), lambda i: (i, 0)),
                   pl.BlockSpec((tm, dkv), lambda i: (i, 0)), *cast_out),
        compiler_params=_params(),
        name="qkv_a",
    )(x, mod.arr, g.arr, w.arr, *cast_args)


B_DQ = B_Q_HEADS * HEAD_DIM
B_DKV = 2 * B_KV_HEADS * HEAD_DIM
B_DG = B_DQ + B_DKV


def _qkv_b_kernel(x_ref, mod_ref, g_ref, w_ref, *refs):
    out_refs, slab_ref = refs[:-1], refs[-1]
    tm = x_ref.shape[0]
    h = _norm_mod(x_ref[...], g_ref[...], mod_ref[0], mod_ref[1])
    hb = h.astype(BF16)
    nq = B_DQ // LANES
    for gi, (_, dil) in enumerate(B_GROUPS):
        q_ref, kv_ref = out_refs[2 * gi], out_refs[2 * gi + 1]
        r = jnp.dot(hb, w_ref[:, gi * B_DG:(gi + 1) * B_DG], preferred_element_type=F32)
        rq = r[:, :B_DQ] * Q_SCALE
        rkv = r[:, B_DQ:]
        if dil == 1:
            q_ref[0] = rq.astype(BF16)
            kv_ref[0] = rkv.astype(BF16)
            continue
        for c in range(B_DG // LANES):
            src = rq if c < nq else rkv
            cc = c if c < nq else c - nq
            slab_ref[c] = src[:, cc * LANES:(cc + 1) * LANES]
        rows = tm // dil
        for rr in range(dil):
            for c in range(B_DG // LANES):
                v = slab_ref[c, pl.ds(rr, rows, stride=dil), :].astype(BF16)
                if c < nq:
                    q_ref[rr, :, c * LANES:(c + 1) * LANES] = v
                else:
                    kv_ref[rr, :, (c - nq) * LANES:(c - nq + 1) * LANES] = v


def _qkv_b(x, mod, g, w):
    s, d = x.shape
    tm = 1024
    out_shape, out_specs = [], []
    for _, dil in B_GROUPS:
        for width in (B_DQ, B_DKV):
            out_shape.append(jax.ShapeDtypeStruct((dil, s // dil, width), BF16))
            out_specs.append(pl.BlockSpec((dil, tm // dil, width), lambda i: (0, i, 0)))
    return pl.pallas_call(
        _qkv_b_kernel,
        out_shape=tuple(out_shape),
        grid=(s // tm,),
        in_specs=[
            pl.BlockSpec((tm, d), lambda i: (i, 0)),
            _layer_spec(mod),
            _layer_spec(g),
            _layer_spec(w),
        ],
        out_specs=tuple(out_specs),
        scratch_shapes=[pltpu.VMEM((B_DG // LANES, tm, LANES), F32)],
        compiler_params=_params(),
        name="qkv_b",
    )(x, mod.arr, g.arr, w.arr)


def _attn_units(q_ref, kv_refs, sink_ref, o_ref, lse_ref, bufs, bias_ref, tile, first_step,
                *, n_kv, halo, seq_len, dist_scale, slopes, sink_row):
    kvp_ref, kvc_ref, kvn_ref = kv_refs
    has_sink = sink_row is not None
    emit_lse = lse_ref is not None
    ts = q_ref.shape[0]
    w = QBLK + 2 * halo
    nh = n_kv * GQ

    @pl.when(first_step)
    def _():
        kj = lax.broadcasted_iota(jnp.int32, (w, QBLK), 0)
        qi = lax.broadcasted_iota(jnp.int32, (w, QBLK), 1)
        rel = jnp.abs(kj - halo - qi)
        dist = (dist_scale * rel).astype(F32)
        for h, slope in enumerate(slopes):
            bias_ref[h] = jnp.where(rel <= halo, ((-float(slope)) * dist) * LOG2E, NEG)
        bias_ref[nh] = jnp.full((w, QBLK), NEG, F32)

    kvbuf, vt_ref = bufs
    kvbuf[0:halo, :] = kvp_ref[...]
    kvbuf[halo:halo + ts, :] = kvc_ref[...]
    kvbuf[halo + ts:, :] = kvn_ref[...]
    dkv = n_kv * HEAD_DIM
    for pair in range(n_kv // 2):
        vt_ref[pair * LANES:(pair + 1) * LANES, :] = (
            kvbuf[:, dkv + pair * LANES:dkv + (pair + 1) * LANES].T)

    ones = jnp.ones((ONES_COLS, w), BF16)
    nt_dims = (((1,), (1,)), ((), ()))

    def scores(r0, kvh):
        c0 = kvh * GQ * HEAD_DIM
        qg = q_ref[pl.ds(r0, QBLK), c0:c0 + GQ * HEAD_DIM]
        qs = jnp.concatenate(
            [qg[:, g * HEAD_DIM:(g + 1) * HEAD_DIM] for g in range(GQ)], axis=0)
        k = kvbuf[pl.ds(r0, w), kvh * HEAD_DIM:(kvh + 1) * HEAD_DIM]
        return lax.dot_general(k, qs, nt_dims, preferred_element_type=F32)

    def softmax(st, r0, kvh, gs):
        seq_pos = (tile * ts + r0) & (seq_len - 1)
        prev_out = seq_pos == 0
        next_out = seq_pos + QBLK == seq_len
        ps, ms = [], []
        for g in gs:
            h = kvh * GQ + g
            top = jnp.where(prev_out, nh, h)
            bot = jnp.where(next_out, nh, h)
            bias = jnp.concatenate([bias_ref[top, 0:halo, :],
                                    bias_ref[h, halo:halo + QBLK, :],
                                    bias_ref[bot, halo + QBLK:w, :]], axis=0)
            sg = st[:, g * QBLK:(g + 1) * QBLK] + bias
            m = jnp.max(sg, axis=0, keepdims=True)
            if has_sink:
                m = jnp.maximum(m, sink_ref[sink_row, h] * LOG2E)
            ps.append(jnp.exp2(sg - m).astype(BF16))
            ms.append(m)
        return jnp.concatenate(ps, axis=1), ms

    def weighted_values(pt, ms, r0, kvh, gs):
        vt = vt_ref[kvh * HEAD_DIM:(kvh + 1) * HEAD_DIM, r0:r0 + w]
        vt_ext = jnp.concatenate([vt, ones], axis=0)
        ot = jnp.dot(vt_ext, pt, preferred_element_type=F32)
        outs, lses = [], []
        for n, g in enumerate(gs):
            h = kvh * GQ + g
            cs = slice(n * QBLK, (n + 1) * QBLK)
            l = ot[HEAD_DIM:HEAD_DIM + 1, cs]
            if has_sink:
                l = l + jnp.exp2(sink_ref[sink_row, h] * LOG2E - ms[n])
            outs.append(ot[0:HEAD_DIM, cs] * (1.0 / l))
            if emit_lse:
                lses.append(jnp.broadcast_to(ms[n] * LN2 + jnp.log(l), (HEAD_DIM, QBLK)))
        for n in range(0, len(gs), 2):
            c0 = (kvh * GQ + gs[n]) * HEAD_DIM
            o2 = jnp.concatenate(outs[n:n + 2], axis=0)
            o_ref[r0:r0 + QBLK, c0:c0 + LANES] = o2.T.astype(o_ref.dtype)
            if emit_lse:
                l2 = jnp.concatenate(lses[n:n + 2], axis=0)
                lse_ref[r0:r0 + QBLK, c0:c0 + LANES] = l2.T

    units = [(qb * QBLK, kvh) for qb in range(ts // QBLK) for kvh in range(n_kv)]
    head_sets = [tuple(range(i, i + HEADS_PER_PV)) for i in range(0, GQ, HEADS_PER_PV)]
    pending, probs = {}, {}

    def issue(u):
        pending[u] = scores(*units[u])

    def soft(u):
        st = pending.pop(u)
        probs[u] = [softmax(st, *units[u], gs) for gs in head_sets]

    def finish(u):
        for gs, (pt, ms) in zip(head_sets, probs.pop(u)):
            weighted_values(pt, ms, *units[u], gs)

    return [(functools.partial(issue, u), functools.partial(soft, u), functools.partial(finish, u))
            for u in range(len(units))]


def _attn_kernel(*refs, emit_lse, pv_delay, **cfg):
    refs = list(refs)
    q_ref, kv_refs = refs[0], refs[1:4]
    pos = 4
    sink_ref = None
    if cfg["sink_row"] is not None:
        sink_ref = refs[pos]
        pos += 1
    o_ref = refs[pos]
    pos += 1
    lse_ref = None
    if emit_lse:
        lse_ref = refs[pos]
        pos += 1
    kvbufs, bias_ref = refs[pos:pos + 2], refs[pos + 2]
    step = pl.program_id(0)
    units = _attn_units(q_ref, kv_refs, sink_ref, o_ref, lse_ref, kvbufs, bias_ref,
                        step, step == 0, **cfg)
    units[0][0]()
    for u, (_, soft, _) in enumerate(units):
        if u + 1 < len(units):
            units[u + 1][0]()
        soft()
        if u >= pv_delay:
            units[u - pv_delay][2]()
    for u in range(max(len(units) - pv_delay, 0), len(units)):
        units[u][2]()


def _attn_in_specs(s, ts, dq, dkv, halo):
    per = ts // halo
    last_halo = s // halo - 1
    return [
        pl.BlockSpec((ts, dq), lambda i: (i, 0)),
        pl.BlockSpec((halo, dkv), lambda i: (jnp.maximum(i * per - 1, 0), 0)),
        pl.BlockSpec((ts, dkv), lambda i: (i, 0)),
        pl.BlockSpec((halo, dkv), lambda i: (jnp.minimum((i + 1) * per, last_halo), 0)),
    ]


def _attn_scratch(ts, dkv, n_kv, halo):
    return [pltpu.VMEM((ts + 2 * halo, dkv), BF16),
            pltpu.VMEM((n_kv * HEAD_DIM, ts + 2 * halo), BF16),
            pltpu.VMEM((n_kv * GQ + 1, QBLK + 2 * halo, QBLK), F32)]


def _attention(q, kv, *, n_kv, halo, seq_len, dist_scale, slopes, pv_delay, units_per_step,
               sink=None, emit_lse=False, out_dtype=BF16):
    s, dq = q.shape
    dkv = kv.shape[1]
    ts = QBLK * (units_per_step // n_kv)
    assert s % ts == 0 and (ts % seq_len == 0 or seq_len % ts == 0)
    assert ts % halo == 0 and dq == n_kv * GQ * HEAD_DIM
    assert dkv == 2 * n_kv * HEAD_DIM
    in_specs = _attn_in_specs(s, ts, dq, dkv, halo)
    args = [q, kv, kv, kv]
    if sink is not None:
        in_specs.append(pl.BlockSpec(memory_space=pltpu.SMEM))
        args.append(sink.arr)
    out_shape = [jax.ShapeDtypeStruct((s, dq), out_dtype)]
    out_specs = [pl.BlockSpec((ts, dq), lambda i: (i, 0))]
    if emit_lse:
        out_shape.append(jax.ShapeDtypeStruct((s, dq), F32))
        out_specs.append(pl.BlockSpec((ts, dq), lambda i: (i, 0)))
    kern = functools.partial(
        _attn_kernel, emit_lse=emit_lse, pv_delay=pv_delay, n_kv=n_kv, halo=halo, seq_len=seq_len,
        dist_scale=dist_scale, slopes=tuple(float(x) for x in slopes),
        sink_row=None if sink is None else sink.idx)
    return pl.pallas_call(
        kern,
        out_shape=tuple(out_shape),
        grid=(s // ts,),
        in_specs=in_specs,
        out_specs=tuple(out_specs),
        scratch_shapes=_attn_scratch(ts, dkv, n_kv, halo),
        compiler_params=_params(),
        name="banded_attn",
    )(*args)


def _silu(x):
    return x * jax.nn.sigmoid(x)


def _post_tail(get_o, x_ref, mod_ref, g_ref, wo_ref, wi_ref, wo2_ref, fin_ref, out_ref):
    tm = x_ref.shape[0]
    rows = tm // POST_ROW_SLICES
    dff = wo2_ref.shape[0]
    chunks = ((0, FFN_SPLIT), (FFN_SPLIT, dff))

    def gate_up(st, c):
        lo, hi = chunks[c]
        st["gu"] = (jnp.dot(st["h2"], wi_ref[:, lo:hi], preferred_element_type=F32),
                    jnp.dot(st["h2"], wi_ref[:, dff + lo:dff + hi], preferred_element_type=F32))

    def down(st, c):
        lo, hi = chunks[c]
        gate, up = st["gu_prev"]
        act = (_silu(gate) * up).astype(BF16)
        part = jnp.dot(act, wo2_ref[lo:hi, :], preferred_element_type=F32)
        st["acc"] = part if c == 0 else st["acc"] + part

    def stage(st, k):
        if k == 0:
            st["y"] = jnp.dot(get_o(st["rs"]), wo_ref[...], preferred_element_type=F32)
        elif k == 1:
            st["x1"] = x_ref[st["rs"], :] + mod_ref[2] * st["y"]
            st["h2"] = _norm_mod(st["x1"], g_ref[...], mod_ref[3],
                                 mod_ref[4]).astype(BF16)
            gate_up(st, 0)
        elif k <= len(chunks):
            st["gu_prev"] = st["gu"]
            gate_up(st, k - 1)
            down(st, k - 2)
        else:
            st["gu_prev"] = st["gu"]
            down(st, len(chunks) - 1)
            x2 = st["x1"] + mod_ref[5] * st["acc"]
            if fin_ref is not None:
                ms = jnp.mean(x2 * x2, axis=-1, keepdims=True)
                x2 = (x2 * lax.rsqrt(ms + RMS_EPS)) * fin_ref[...]
            out_ref[st["rs"], :] = x2

    states = [{"rs": slice(n * rows, (n + 1) * rows)} for n in range(POST_ROW_SLICES)]
    for k in range(len(chunks) + 2):
        for st in states:
            stage(st, k)


def _merged_o(attn_refs, slabs, om_ref):
    o0_ref, l0_ref, o1_ref, l1_ref, o2_ref, l2_ref = attn_refs
    so1, sl1, so2, sl2 = slabs
    nc = B_DQ // LANES

    def merged(rs):
        for src_o, src_l, dst_o, dst_l in ((o1_ref, l1_ref, so1, sl1), (o2_ref, l2_ref, so2, sl2)):
            dil = src_o.shape[0]
            n_src = (rs.stop - rs.start) // dil
            src_rs = slice(rs.start // dil, rs.start // dil + n_src)
            for rr in range(dil):
                dst_rows = pl.ds(rs.start + rr, n_src, stride=dil)
                for c in range(nc):
                    cs = slice(c * LANES, (c + 1) * LANES)
                    dst_o[c, dst_rows, :] = src_o[rr, src_rs, cs]
                    dst_l[c, dst_rows, :] = src_l[rr, src_rs, cs]
        for c in range(nc):
            cs = slice(c * LANES, (c + 1) * LANES)
            l0, l1, l2 = l0_ref[0, rs, cs], sl1[c, rs, :], sl2[c, rs, :]
            mx = jnp.maximum(jnp.maximum(l0, l1), l2)
            w0, w1, w2 = jnp.exp(l0 - mx), jnp.exp(l1 - mx), jnp.exp(l2 - mx)
            num = w0 * o0_ref[0, rs, cs] + w1 * so1[c, rs, :] + w2 * so2[c, rs, :]
            om_ref[rs, cs] = (num / (w0 + w1 + w2)).astype(BF16)
        return om_ref[rs, :]

    return merged


def _post_kernel(*refs, n_attn, final, cast_cols, has_ada):
    refs = list(refs)
    attn_refs = refs[:n_attn]
    x_ref, mod_ref, g_ref, wo_ref, wi_ref, wo2_ref = refs[n_attn:n_attn + 6]
    pos = n_attn + 6
    fin_ref = None
    if final:
        fin_ref = refs[pos]
        pos += 1
    n_cast = len(cast_cols)
    cast_src = refs[pos:pos + n_cast]
    pos += n_cast
    ada_in = refs[pos:pos + 3] if has_ada else None
    pos += 3 if has_ada else 0
    out_ref = refs[pos]
    cast_dst = refs[pos + 1:pos + 1 + n_cast]
    pos += 1 + n_cast
    ada_out = refs[pos] if has_ada else None
    pos += 1 if has_ada else 0
    scratch = refs[pos:]

    _run_casts(cast_src, cast_dst, cast_cols)
    if has_ada:
        _ada_block(*ada_in, ada_out)

    if n_attn == 1:
        (o_ref,) = attn_refs
        get_o = lambda rs: o_ref[rs, :]
    else:
        get_o = _merged_o(attn_refs, scratch[:4], scratch[4])
    _post_tail(get_o, x_ref, mod_ref, g_ref, wo_ref, wi_ref, wo2_ref, fin_ref, out_ref)


def _post(o_parts, x, mod, wo, g, wi, wo2, final_g, *, casts=(), next_ada=None):
    s, d = x.shape
    tm = 512
    steps = s // tm
    final = final_g is not None
    dilated = len(o_parts) > 1
    in_specs, args, scratch = [], [], []
    if dilated:
        for arr in o_parts:
            dil = arr.shape[0]
            in_specs.append(pl.BlockSpec((dil, tm // dil, arr.shape[2]), lambda i: (0, i, 0)))
            args.append(arr)
        scratch = [pltpu.VMEM((B_DQ // LANES, tm, LANES), F32) for _ in range(4)]
        scratch.append(pltpu.VMEM((tm, B_DQ), BF16))
    else:
        (o,) = o_parts
        in_specs.append(pl.BlockSpec((tm, o.shape[1]), lambda i: (i, 0)))
        args.append(o)
    params = [mod, g, wo, wi, wo2] + ([final_g] if final else [])
    in_specs += [pl.BlockSpec((tm, d), lambda i: (i, 0))] + [_layer_spec(p) for p in params]
    args += [x] + [p.arr for p in params]
    out_shape = [jax.ShapeDtypeStruct((s, d), F32)]
    out_specs = [pl.BlockSpec((tm, d), lambda i: (i, 0))]
    cast_in, cast_args, cast_shape, cast_out = _cast_specs(casts, steps)
    in_specs += cast_in
    args += cast_args
    out_shape += cast_shape
    out_specs += cast_out
    if next_ada is not None:
        c_col, ada_w, ada_b3, layer = next_ada
        n = ada_w.shape[2]
        n_blk = next(k for k in range(steps, 0, -1) if n % k == 0 and (n // k) % LANES == 0)
        tn, last = n // n_blk, n_blk - 1
        ada_in, ada_out, ada_shape = _ada_specs(ada_w, layer, tn,
                                                lambda i: jnp.minimum(i, last))
        in_specs += ada_in
        args += [c_col, ada_w, ada_b3]
        out_shape.append(ada_shape)
        out_specs.append(ada_out)
    kern = functools.partial(
        _post_kernel, n_attn=len(o_parts), final=final,
        cast_cols=tuple(cst.cols for cst in casts), has_ada=next_ada is not None)
    return pl.pallas_call(
        kern,
        out_shape=tuple(out_shape),
        grid=(steps,),
        in_specs=in_specs,
        out_specs=tuple(out_specs),
        scratch_shapes=scratch,
        compiler_params=_params(),
        name="post_b" if dilated else "post_a",
    )(*args)


def _b_w_in_cols():
    ng = len(B_GROUPS)
    dq, dk = ng * B_DQ, ng * B_KV_HEADS * HEAD_DIM
    dkg = B_KV_HEADS * HEAD_DIM
    cols = []
    for gi in range(ng):
        cols += [(gi * B_DQ, B_DQ), (dq + gi * dkg, dkg), (dq + dk + gi * dkg, dkg)]
    return tuple(cols)


def kernel(x, c, ada_w, ada_b, norm_mix, norm_ffn, ffn_w_in, ffn_w_out,
           a_w_in, a_w_out, a_sink, b_w_in, b_w_out, final_norm):
    b, s, d = x.shape
    assert b == 1
    depth = ada_w.shape[0]
    xs = x.reshape(s, d)
    c_col = c.reshape(d, 1)
    ada_b3 = ada_b.reshape(depth, 1, -1)
    g_mix = norm_mix.reshape(depth, 1, d)
    g_ffn = norm_ffn.reshape(depth, 1, d)
    slopes_a = _alibi_slopes(A_Q_HEADS)
    slopes_b = _alibi_slopes(len(B_GROUPS) * B_Q_HEADS)
    b_cols = _b_w_in_cols()

    def mixer_params(i):
        return (a_w_in, a_w_out, i // 2, None) if i % 2 == 0 else (b_w_in, b_w_out, i // 2, b_cols)

    mod = _ada_mod(c_col, ada_w, ada_b3, 0)
    w_in = _Layer(a_w_in[0].astype(BF16)[None], 0)
    casts0 = (_Cast(a_w_out, 0), _Cast(ffn_w_in, 0), _Cast(ffn_w_out, 0))
    for i in range(depth):
        mod_l = _Layer(mod[None], 0)
        if i % 2 == 0:
            q, kv, *cast0 = _qkv_a(xs, mod_l, _Layer(g_mix, i), w_in, casts0 if i == 0 else ())
            if i == 0:
                w_out, wi, wo2 = (_Layer(r[None], 0) for r in cast0)
            o_parts = _attention(q, kv, n_kv=A_KV_HEADS, halo=A_HALF_WINDOW, seq_len=s,
                                 dist_scale=1, slopes=slopes_a, pv_delay=A_PV_DELAY,
                                 units_per_step=A_UNITS_PER_STEP,
                                 sink=_Layer(a_sink, i // 2))
        else:
            outs = _qkv_b(xs, mod_l, _Layer(g_mix, i), w_in)
            o_parts = []
            for gi, (window, dil) in enumerate(B_GROUPS):
                qg = outs[2 * gi].reshape(s, B_DQ)
                kvg = outs[2 * gi + 1].reshape(s, B_DKV)
                og, lg = _attention(
                    qg, kvg, n_kv=B_KV_HEADS, halo=window // (2 * dil), seq_len=s // dil,
                    dist_scale=dil, slopes=slopes_b[gi * B_Q_HEADS:(gi + 1) * B_Q_HEADS],
                    pv_delay=B_PV_DELAY, units_per_step=B_UNITS_PER_STEP,
                    emit_lse=True, out_dtype=F32)
                o_parts += [og.reshape(dil, s // dil, B_DQ), lg.reshape(dil, s // dil, B_DQ)]
        last = i == depth - 1
        casts, next_ada = (), None
        if not last:
            m_in, m_out, j, cols = mixer_params(i + 1)
            casts = (_Cast(m_in, j, cols), _Cast(m_out, j),
                     _Cast(ffn_w_in, i + 1), _Cast(ffn_w_out, i + 1))
            next_ada = (c_col, ada_w, ada_b3, i + 1)
        fin = _Layer(final_norm.reshape(1, 1, d), 0) if last else None
        res = _post(o_parts, xs, mod_l, w_out, _Layer(g_ffn, i), wi, wo2, fin,
                    casts=casts, next_ada=next_ada)
        xs = res[0]
        if not last:
            w_in, w_out, wi, wo2 = (_Layer(r[None], 0) for r in res[1:5])
            mod = res[5]
    return xs.reshape(b, s, d)
```

```python
import functools
from typing import NamedTuple

import numpy as np
import jax
import jax.numpy as jnp
from jax import lax
from jax.experimental import pallas as pl
from jax.experimental.pallas import tpu as pltpu

HEAD_DIM = 64
A_Q_HEADS = 16
A_KV_HEADS = 4
A_HALF_WINDOW = 128
B_GROUPS = ((128, 1), (512, 4), (2048, 16))
B_Q_HEADS = 8
B_KV_HEADS = 2
GQ = 4
RMS_EPS = 1e-6
NEG = -1e30
LOG2E = 1.4426950408889634
LN2 = 0.6931471805599453
Q_SCALE = HEAD_DIM ** -0.5 * LOG2E

LANES = 128
QBLK = 128
A_UNITS_PER_STEP, B_UNITS_PER_STEP = 64, 64
ONES_COLS = 32
HEADS_PER_PV = 4
A_PV_DELAY, B_PV_DELAY = 0, 1
FFN_SPLIT = 1536
QKV_ROW_SLICES = 2
POST_ROW_SLICES = 2
VMEM_LIMIT = 56 * 1024 * 1024

BF16 = jnp.bfloat16
F32 = jnp.float32


def _alibi_slopes(n):
    return np.asarray(2.0 ** (-8.0 * np.arange(1, n + 1) / n), dtype=np.float32)


class _Layer(NamedTuple):
    arr: jax.Array
    idx: int


def _layer_spec(p):
    nd = p.arr.ndim
    return pl.BlockSpec((None,) + p.arr.shape[1:], lambda *_: (p.idx,) + (0,) * (nd - 1),
                        pipeline_mode=pl.Buffered(1))


def _params(n_axes=1, **kwargs):
    return pltpu.CompilerParams(
        dimension_semantics=("arbitrary",) * n_axes, vmem_limit_bytes=VMEM_LIMIT, **kwargs)


def _ada_block(c_ref, w_ref, b_ref, o_ref):
    c = c_ref[...]
    cond = c * jax.nn.sigmoid(c)
    o_ref[...] = jnp.sum(w_ref[...] * cond, axis=0, keepdims=True) + b_ref[...]


def _ada_specs(ada_w, layer, tn, col_block):
    d, n = ada_w.shape[1:]
    per = d // tn
    assert d % tn == 0 and n % d == 0
    return ([pl.BlockSpec((d, 1), lambda *i: (0, 0)),
             pl.BlockSpec((None, d, tn), lambda *i: (layer, 0, col_block(*i))),
             pl.BlockSpec((None, 1, tn), lambda *i: (layer, 0, col_block(*i)))],
            pl.BlockSpec((None, 1, tn), lambda *i: (col_block(*i) // per, 0, col_block(*i) % per)),
            jax.ShapeDtypeStruct((n // d, 1, d), F32))


def _ada_mod(c_col, ada_w, ada_b3, layer):
    d, n = ada_w.shape[1:]
    in_specs, out_spec, out_shape = _ada_specs(ada_w, layer, d, lambda j: j)
    return pl.pallas_call(
        _ada_block,
        out_shape=out_shape,
        grid=(n // d,),
        in_specs=in_specs,
        out_specs=out_spec,
        compiler_params=_params(),
        name="ada_mod",
    )(c_col, ada_w, ada_b3)


def _norm_mod(x, g, shift, scale):
    ms = jnp.mean(x * x, axis=-1, keepdims=True)
    y = x * lax.rsqrt(ms + RMS_EPS)
    return y * (g * (1.0 + scale)) + shift


class _Cast(NamedTuple):
    arr: jax.Array
    idx: int
    cols: tuple | None = None


BF16_SUBLANES = 16


def _cast_specs(casts, steps):
    in_specs, args, out_shape, out_specs = [], [], [], []
    for cst in casts:
        _, n_rows, n_cols = cst.arr.shape
        n_blk = next(n for n in range(steps, 0, -1)
                     if n_rows % n == 0 and (n_rows // n) % BF16_SUBLANES == 0)
        rows, last = n_rows // n_blk, n_blk - 1
        in_specs.append(pl.BlockSpec((None, rows, n_cols),
                                     lambda i, cst=cst, last=last: (cst.idx, jnp.minimum(i, last), 0)))
        args.append(cst.arr)
        out_shape.append(jax.ShapeDtypeStruct((n_rows, n_cols), BF16))
        out_specs.append(pl.BlockSpec((rows, n_cols),
                                      lambda i, last=last: (jnp.minimum(i, last), 0)))
    return in_specs, args, out_shape, out_specs


def _run_casts(srcs, dsts, cast_cols):
    for src, dst, cols in zip(srcs, dsts, cast_cols):
        if cols is None:
            dst[...] = src[...].astype(BF16)
        else:
            at = 0
            for start, width in cols:
                dst[:, at:at + width] = src[:, start:start + width].astype(BF16)
                at += width


def _qkv_a_kernel(x_ref, mod_ref, g_ref, w_ref, *refs, cast_cols):
    n_cast = len(cast_cols)
    cast_src, (q_ref, kv_ref), cast_dst = refs[:n_cast], refs[n_cast:n_cast + 2], refs[n_cast + 2:]
    _run_casts(cast_src, cast_dst, cast_cols)
    dq = q_ref.shape[1]
    rows = x_ref.shape[0] // QKV_ROW_SLICES
    for n in range(QKV_ROW_SLICES):
        rs = slice(n * rows, (n + 1) * rows)
        h = _norm_mod(x_ref[rs, :], g_ref[...], mod_ref[0], mod_ref[1])
        r = jnp.dot(h.astype(BF16), w_ref[...], preferred_element_type=F32)
        q_ref[rs, :] = (r[:, :dq] * Q_SCALE).astype(BF16)
        kv_ref[rs, :] = r[:, dq:].astype(BF16)


def _qkv_a(x, mod, g, w, casts=()):
    s, d = x.shape
    dq = A_Q_HEADS * HEAD_DIM
    dkv = w.arr.shape[2] - dq
    tm = 1024
    cast_in, cast_args, cast_shape, cast_out = _cast_specs(casts, s // tm)
    return pl.pallas_call(
        functools.partial(_qkv_a_kernel, cast_cols=tuple(cst.cols for cst in casts)),
        out_shape=(jax.ShapeDtypeStruct((s, dq), BF16),
                   jax.ShapeDtypeStruct((s, dkv), BF16), *cast_shape),
        grid=(s // tm,),
        in_specs=[
            pl.BlockSpec((tm, d), lambda i: (i, 0)),
            _layer_spec(mod),
            _layer_spec(g),
            _layer_spec(w),
            *cast_in,
        ],
        out_specs=(pl.BlockSpec((tm, dq), lambda i: (i, 0)),
                   pl.BlockSpec((tm, dkv), lambda i: (i, 0)), *cast_out),
        compiler_params=_params(),
        name="qkv_a",
    )(x, mod.arr, g.arr, w.arr, *cast_args)


B_DQ = B_Q_HEADS * HEAD_DIM
B_DKV = 2 * B_KV_HEADS * HEAD_DIM
B_DG = B_DQ + B_DKV


def _qkv_b_kernel(x_ref, mod_ref, g_ref, w_ref, *out_refs):
    h = _norm_mod(x_ref[...], g_ref[...], mod_ref[0], mod_ref[1])
    hb = h.astype(BF16)
    for gi in range(len(B_GROUPS)):
        q_ref, kv_ref = out_refs[2 * gi], out_refs[2 * gi + 1]
        r = jnp.dot(hb, w_ref[:, gi * B_DG:(gi + 1) * B_DG], preferred_element_type=F32)
        q_ref[...] = (r[:, :B_DQ] * Q_SCALE).astype(BF16)
        kv_ref[...] = r[:, B_DQ:].astype(BF16)


def _qkv_b(x, mod, g, w):
    s, d = x.shape
    tm = 1024
    out_shape, out_specs = [], []
    for _ in B_GROUPS:
        for width in (B_DQ, B_DKV):
            out_shape.append(jax.ShapeDtypeStruct((s, width), BF16))
            out_specs.append(pl.BlockSpec((tm, width), lambda i: (i, 0)))
    return pl.pallas_call(
        _qkv_b_kernel,
        out_shape=tuple(out_shape),
        grid=(s // tm,),
        in_specs=[
            pl.BlockSpec((tm, d), lambda i: (i, 0)),
            _layer_spec(mod),
            _layer_spec(g),
            _layer_spec(w),
        ],
        out_specs=tuple(out_specs),
        compiler_params=_params(),
        name="qkv_b",
    )(x, mod.arr, g.arr, w.arr)


def _attn_units(q_ref, kv_refs, sink_ref, o_ref, lse_ref, bufs, bias_ref, tile, first_step,
                *, n_kv, halo, seq_len, dist_scale, slopes, sink_row):
    kvp_ref, kvc_ref, kvn_ref = kv_refs
    has_sink = sink_row is not None
    emit_lse = lse_ref is not None
    ts = q_ref.shape[0]
    w = QBLK + 2 * halo
    nh = n_kv * GQ

    @pl.when(first_step)
    def _():
        kj = lax.broadcasted_iota(jnp.int32, (w, QBLK), 0)
        qi = lax.broadcasted_iota(jnp.int32, (w, QBLK), 1)
        rel = jnp.abs(kj - halo - qi)
        dist = (dist_scale * rel).astype(F32)
        for h, slope in enumerate(slopes):
            bias_ref[h] = jnp.where(rel <= halo, ((-float(slope)) * dist) * LOG2E, NEG)
        bias_ref[nh] = jnp.full((w, QBLK), NEG, F32)

    kvbuf, vt_ref = bufs
    kvbuf[0:halo, :] = kvp_ref[...]
    kvbuf[halo:halo + ts, :] = kvc_ref[...]
    kvbuf[halo + ts:, :] = kvn_ref[...]
    dkv = n_kv * HEAD_DIM
    for pair in range(n_kv // 2):
        vt_ref[pair * LANES:(pair + 1) * LANES, :] = (
            kvbuf[:, dkv + pair * LANES:dkv + (pair + 1) * LANES].T)

    ones = jnp.ones((ONES_COLS, w), BF16)
    nt_dims = (((1,), (1,)), ((), ()))

    def scores(r0, kvh):
        c0 = kvh * GQ * HEAD_DIM
        qg = q_ref[pl.ds(r0, QBLK), c0:c0 + GQ * HEAD_DIM]
        qs = jnp.concatenate(
            [qg[:, g * HEAD_DIM:(g + 1) * HEAD_DIM] for g in range(GQ)], axis=0)
        k = kvbuf[pl.ds(r0, w), kvh * HEAD_DIM:(kvh + 1) * HEAD_DIM]
        return lax.dot_general(k, qs, nt_dims, preferred_element_type=F32)

    def softmax(st, r0, kvh, gs):
        seq_pos = (tile * ts + r0) & (seq_len - 1)
        prev_out = seq_pos == 0
        next_out = seq_pos + QBLK == seq_len
        ps, ms = [], []
        for g in gs:
            h = kvh * GQ + g
            top = jnp.where(prev_out, nh, h)
            bot = jnp.where(next_out, nh, h)
            bias = jnp.concatenate([bias_ref[top, 0:halo, :],
                                    bias_ref[h, halo:halo + QBLK, :],
                                    bias_ref[bot, halo + QBLK:w, :]], axis=0)
            sg = st[:, g * QBLK:(g + 1) * QBLK] + bias
            m = jnp.max(sg, axis=0, keepdims=True)
            if has_sink:
                m = jnp.maximum(m, sink_ref[sink_row, h] * LOG2E)
            ps.append(jnp.exp2(sg - m).astype(BF16))
            ms.append(m)
        return jnp.concatenate(ps, axis=1), ms

    def weighted_values(pt, ms, r0, kvh, gs):
        vt = vt_ref[kvh * HEAD_DIM:(kvh + 1) * HEAD_DIM, r0:r0 + w]
        vt_ext = jnp.concatenate([vt, ones], axis=0)
        ot = jnp.dot(vt_ext, pt, preferred_element_type=F32)
        outs, lses = [], []
        for n, g in enumerate(gs):
            h = kvh * GQ + g
            cs = slice(n * QBLK, (n + 1) * QBLK)
            l = ot[HEAD_DIM:HEAD_DIM + 1, cs]
            if has_sink:
                l = l + jnp.exp2(sink_ref[sink_row, h] * LOG2E - ms[n])
            outs.append(ot[0:HEAD_DIM, cs] * (1.0 / l))
            if emit_lse:
                lses.append(jnp.broadcast_to(ms[n] * LN2 + jnp.log(l), (HEAD_DIM, QBLK)))
        for n in range(0, len(gs), 2):
            c0 = (kvh * GQ + gs[n]) * HEAD_DIM
            o2 = jnp.concatenate(outs[n:n + 2], axis=0)
            o_ref[r0:r0 + QBLK, c0:c0 + LANES] = o2.T.astype(o_ref.dtype)
            if emit_lse:
                l2 = jnp.concatenate(lses[n:n + 2], axis=0)
                lse_ref[r0:r0 + QBLK, c0:c0 + LANES] = l2.T

    units = [(qb * QBLK, kvh) for qb in range(ts // QBLK) for kvh in range(n_kv)]
    head_sets = [tuple(range(i, i + HEADS_PER_PV)) for i in range(0, GQ, HEADS_PER_PV)]
    pending, probs = {}, {}

    def issue(u):
        pending[u] = scores(*units[u])

    def soft(u):
        st = pending.pop(u)
        probs[u] = [softmax(st, *units[u], gs) for gs in head_sets]

    def finish(u):
        for gs, (pt, ms) in zip(head_sets, probs.pop(u)):
            weighted_values(pt, ms, *units[u], gs)

    return [(functools.partial(issue, u), functools.partial(soft, u), functools.partial(finish, u))
            for u in range(len(units))]


def _attn_kernel(*refs, emit_lse, pv_delay, **cfg):
    refs = list(refs)
    q_ref, kv_refs = refs[0], refs[1:4]
    pos = 4
    sink_ref = None
    if cfg["sink_row"] is not None:
        sink_ref = refs[pos]
        pos += 1
    o_ref = refs[pos]
    pos += 1
    lse_ref = None
    if emit_lse:
        lse_ref = refs[pos]
        pos += 1
    kvbufs, bias_ref = refs[pos:pos + 2], refs[pos + 2]
    step = pl.program_id(0)
    units = _attn_units(q_ref, kv_refs, sink_ref, o_ref, lse_ref, kvbufs, bias_ref,
                        step, step == 0, **cfg)
    units[0][0]()
    for u, (_, soft, _) in enumerate(units):
        if u + 1 < len(units):
            units[u + 1][0]()
        soft()
        if u >= pv_delay:
            units[u - pv_delay][2]()
    for u in range(max(len(units) - pv_delay, 0), len(units)):
        units[u][2]()


def _attn_in_specs(seq_len, ts, dq, dkv, halo):
    tps = seq_len // ts
    per = ts // halo
    last_halo = seq_len // halo - 1
    return [
        pl.BlockSpec((ts, dq), lambda i: (i % tps, i // tps)),
        pl.BlockSpec((halo, dkv), lambda i: (jnp.maximum((i % tps) * per - 1, 0), i // tps)),
        pl.BlockSpec((ts, dkv), lambda i: (i % tps, i // tps)),
        pl.BlockSpec((halo, dkv),
                     lambda i: (jnp.minimum((i % tps + 1) * per, last_halo), i // tps)),
    ]


def _attn_scratch(ts, dkv, n_kv, halo):
    return [pltpu.VMEM((ts + 2 * halo, dkv), BF16),
            pltpu.VMEM((n_kv * HEAD_DIM, ts + 2 * halo), BF16),
            pltpu.VMEM((n_kv * GQ + 1, QBLK + 2 * halo, QBLK), F32)]


def _attention(q, kv, *, n_kv, halo, seq_len, dist_scale, slopes, pv_delay, units_per_step,
               sink=None, emit_lse=False, out_dtype=BF16):
    s, dq = q.shape
    dkv = kv.shape[1]
    n_seq = s // seq_len
    ts = min(QBLK * (units_per_step // n_kv), seq_len)
    tps = seq_len // ts
    assert s == n_seq * seq_len and seq_len % ts == 0
    assert ts % halo == 0 and dq == n_kv * GQ * HEAD_DIM
    assert dkv == 2 * n_kv * HEAD_DIM
    in_specs = _attn_in_specs(seq_len, ts, dq, dkv, halo)
    kv_view = kv.reshape(seq_len, n_seq * dkv)
    args = [q.reshape(seq_len, n_seq * dq), kv_view, kv_view, kv_view]
    if sink is not None:
        in_specs.append(pl.BlockSpec(memory_space=pltpu.SMEM))
        args.append(sink.arr)
    out_shape = [jax.ShapeDtypeStruct((seq_len, n_seq * dq), out_dtype)]
    out_specs = [pl.BlockSpec((ts, dq), lambda i: (i % tps, i // tps))]
    if emit_lse:
        out_shape.append(jax.ShapeDtypeStruct((seq_len, n_seq * dq), F32))
        out_specs.append(pl.BlockSpec((ts, dq), lambda i: (i % tps, i // tps)))
    kern = functools.partial(
        _attn_kernel, emit_lse=emit_lse, pv_delay=pv_delay, n_kv=n_kv, halo=halo, seq_len=seq_len,
        dist_scale=dist_scale, slopes=tuple(float(x) for x in slopes),
        sink_row=None if sink is None else sink.idx)
    outs = pl.pallas_call(
        kern,
        out_shape=tuple(out_shape),
        grid=(s // ts,),
        in_specs=in_specs,
        out_specs=tuple(out_specs),
        scratch_shapes=_attn_scratch(ts, dkv, n_kv, halo),
        compiler_params=_params(),
        name="banded_attn",
    )(*args)
    return tuple(o.reshape(s, dq) for o in outs)


def _silu(x):
    return x * jax.nn.sigmoid(x)


def _post_tail(get_o, x_ref, mod_ref, g_ref, wo_ref, wi_ref, wo2_ref, fin_ref, out_ref):
    tm = x_ref.shape[0]
    rows = tm // POST_ROW_SLICES
    dff = wo2_ref.shape[0]
    chunks = ((0, FFN_SPLIT), (FFN_SPLIT, dff))

    def gate_up(st, c):
        lo, hi = chunks[c]
        st["gu"] = (jnp.dot(st["h2"], wi_ref[:, lo:hi], preferred_element_type=F32),
                    jnp.dot(st["h2"], wi_ref[:, dff + lo:dff + hi], preferred_element_type=F32))

    def down(st, c):
        lo, hi = chunks[c]
        gate, up = st["gu_prev"]
        act = (_silu(gate) * up).astype(BF16)
        part = jnp.dot(act, wo2_ref[lo:hi, :], preferred_element_type=F32)
        st["acc"] = part if c == 0 else st["acc"] + part

    def stage(st, k):
        if k == 0:
            st["y"] = jnp.dot(get_o(st["rs"]), wo_ref[...], preferred_element_type=F32)
        elif k == 1:
            st["x1"] = x_ref[st["rs"], :] + mod_ref[2] * st["y"]
            st["h2"] = _norm_mod(st["x1"], g_ref[...], mod_ref[3],
                                 mod_ref[4]).astype(BF16)
            gate_up(st, 0)
        elif k <= len(chunks):
            st["gu_prev"] = st["gu"]
            gate_up(st, k - 1)
            down(st, k - 2)
        else:
            st["gu_prev"] = st["gu"]
            down(st, len(chunks) - 1)
            x2 = st["x1"] + mod_ref[5] * st["acc"]
            if fin_ref is not None:
                ms = jnp.mean(x2 * x2, axis=-1, keepdims=True)
                x2 = (x2 * lax.rsqrt(ms + RMS_EPS)) * fin_ref[...]
            out_ref[st["rs"], :] = x2

    states = [{"rs": slice(n * rows, (n + 1) * rows)} for n in range(POST_ROW_SLICES)]
    for k in range(len(chunks) + 2):
        for st in states:
            stage(st, k)


def _merged_o(attn_refs, om_ref):
    o0_ref, l0_ref, o1_ref, l1_ref, o2_ref, l2_ref = attn_refs

    def merged(rs):
        for c in range(B_DQ // LANES):
            cs = slice(c * LANES, (c + 1) * LANES)
            l0, l1, l2 = l0_ref[rs, cs], l1_ref[rs, cs], l2_ref[rs, cs]
            mx = jnp.maximum(jnp.maximum(l0, l1), l2)
            w0, w1, w2 = jnp.exp(l0 - mx), jnp.exp(l1 - mx), jnp.exp(l2 - mx)
            num = w0 * o0_ref[rs, cs] + w1 * o1_ref[rs, cs] + w2 * o2_ref[rs, cs]
            om_ref[rs, cs] = (num / (w0 + w1 + w2)).astype(BF16)
        return om_ref[rs, :]

    return merged


def _post_kernel(*refs, n_attn, final, cast_cols, has_ada):
    refs = list(refs)
    attn_refs = refs[:n_attn]
    x_ref, mod_ref, g_ref, wo_ref, wi_ref, wo2_ref = refs[n_attn:n_attn + 6]
    pos = n_attn + 6
    fin_ref = None
    if final:
        fin_ref = refs[pos]
        pos += 1
    n_cast = len(cast_cols)
    cast_src = refs[pos:pos + n_cast]
    pos += n_cast
    ada_in = refs[pos:pos + 3] if has_ada else None
    pos += 3 if has_ada else 0
    out_ref = refs[pos]
    cast_dst = refs[pos + 1:pos + 1 + n_cast]
    pos += 1 + n_cast
    ada_out = refs[pos] if has_ada else None
    pos += 1 if has_ada else 0
    scratch = refs[pos:]

    _run_casts(cast_src, cast_dst, cast_cols)
    if has_ada:
        _ada_block(*ada_in, ada_out)

    if n_attn == 1:
        (o_ref,) = attn_refs
        get_o = lambda rs: o_ref[rs, :]
    else:
        get_o = _merged_o(attn_refs, scratch[0])
    _post_tail(get_o, x_ref, mod_ref, g_ref, wo_ref, wi_ref, wo2_ref, fin_ref, out_ref)


def _post(o_parts, x, mod, wo, g, wi, wo2, final_g, *, casts=(), next_ada=None):
    s, d = x.shape
    tm = 512
    steps = s // tm
    final = final_g is not None
    dilated = len(o_parts) > 1
    in_specs, args, scratch = [], [], []
    if dilated:
        for arr in o_parts:
            in_specs.append(pl.BlockSpec((tm, arr.shape[1]), lambda i: (i, 0)))
            args.append(arr)
        scratch = [pltpu.VMEM((tm, B_DQ), BF16)]
    else:
        (o,) = o_parts
        in_specs.append(pl.BlockSpec((tm, o.shape[1]), lambda i: (i, 0)))
        args.append(o)
    params = [mod, g, wo, wi, wo2] + ([final_g] if final else [])
    in_specs += [pl.BlockSpec((tm, d), lambda i: (i, 0))] + [_layer_spec(p) for p in params]
    args += [x] + [p.arr for p in params]
    out_shape = [jax.ShapeDtypeStruct((s, d), F32)]
    out_specs = [pl.BlockSpec((tm, d), lambda i: (i, 0))]
    cast_in, cast_args, cast_shape, cast_out = _cast_specs(casts, steps)
    in_specs += cast_in
    args += cast_args
    out_shape += cast_shape
    out_specs += cast_out
    if next_ada is not None:
        c_col, ada_w, ada_b3, layer = next_ada
        n = ada_w.shape[2]
        n_blk = next(k for k in range(steps, 0, -1) if n % k == 0 and (n // k) % LANES == 0)
        tn, last = n // n_blk, n_blk - 1
        ada_in, ada_out, ada_shape = _ada_specs(ada_w, layer, tn,
                                                lambda i: jnp.minimum(i, last))
        in_specs += ada_in
        args += [c_col, ada_w, ada_b3]
        out_shape.append(ada_shape)
        out_specs.append(ada_out)
    kern = functools.partial(
        _post_kernel, n_attn=len(o_parts), final=final,
        cast_cols=tuple(cst.cols for cst in casts), has_ada=next_ada is not None)
    return pl.pallas_call(
        kern,
        out_shape=tuple(out_shape),
        grid=(steps,),
        in_specs=in_specs,
        out_specs=tuple(out_specs),
        scratch_shapes=scratch,
        compiler_params=_params(),
        name="post_b" if dilated else "post_a",
    )(*args)


def _b_w_in_cols():
    ng = len(B_GROUPS)
    dq, dk = ng * B_DQ, ng * B_KV_HEADS * HEAD_DIM
    dkg = B_KV_HEADS * HEAD_DIM
    cols = []
    for gi in range(ng):
        cols += [(gi * B_DQ, B_DQ), (dq + gi * dkg, dkg), (dq + dk + gi * dkg, dkg)]
    return tuple(cols)


def kernel(x, c, ada_w, ada_b, norm_mix, norm_ffn, ffn_w_in, ffn_w_out,
           a_w_in, a_w_out, a_sink, b_w_in, b_w_out, final_norm):
    b, s, d = x.shape
    assert b == 1
    depth = ada_w.shape[0]
    xs = x.reshape(s, d)
    c_col = c.reshape(d, 1)
    ada_b3 = ada_b.reshape(depth, 1, -1)
    g_mix = norm_mix.reshape(depth, 1, d)
    g_ffn = norm_ffn.reshape(depth, 1, d)
    slopes_a = _alibi_slopes(A_Q_HEADS)
    slopes_b = _alibi_slopes(len(B_GROUPS) * B_Q_HEADS)
    b_cols = _b_w_in_cols()

    def mixer_params(i):
        return (a_w_in, a_w_out, i // 2, None) if i % 2 == 0 else (b_w_in, b_w_out, i // 2, b_cols)

    mod = _ada_mod(c_col, ada_w, ada_b3, 0)
    w_in = _Layer(a_w_in[0].astype(BF16)[None], 0)
    casts0 = (_Cast(a_w_out, 0), _Cast(ffn_w_in, 0), _Cast(ffn_w_out, 0))
    for i in range(depth):
        mod_l = _Layer(mod[None], 0)
        if i % 2 == 0:
            q, kv, *cast0 = _qkv_a(xs, mod_l, _Layer(g_mix, i), w_in, casts0 if i == 0 else ())
            if i == 0:
                w_out, wi, wo2 = (_Layer(r[None], 0) for r in cast0)
            o_parts = _attention(q, kv, n_kv=A_KV_HEADS, halo=A_HALF_WINDOW, seq_len=s,
                                 dist_scale=1, slopes=slopes_a, pv_delay=A_PV_DELAY,
                                 units_per_step=A_UNITS_PER_STEP,
                                 sink=_Layer(a_sink, i // 2))
        else:
            outs = _qkv_b(xs, mod_l, _Layer(g_mix, i), w_in)
            o_parts = []
            for gi, (window, dil) in enumerate(B_GROUPS):
                o_parts += _attention(
                    outs[2 * gi], outs[2 * gi + 1], n_kv=B_KV_HEADS, halo=window // (2 * dil), seq_len=s // dil,
                    dist_scale=dil, slopes=slopes_b[gi * B_Q_HEADS:(gi + 1) * B_Q_HEADS],
                    pv_delay=B_PV_DELAY, units_per_step=B_UNITS_PER_STEP,
                    emit_lse=True, out_dtype=F32)
        last = i == depth - 1
        casts, next_ada = (), None
        if not last:
            m_in, m_out, j, cols = mixer_params(i + 1)
            casts = (_Cast(m_in, j, cols), _Cast(m_out, j),
                     _Cast(ffn_w_in, i + 1), _Cast(ffn_w_out, i + 1))
            next_ada = (c_col, ada_w, ada_b3, i + 1)
        fin = _Layer(final_norm.reshape(1, 1, d), 0) if last else None
        res = _post(o_parts, xs, mod_l, w_out, _Layer(g_ffn, i), wi, wo2, fin,
                    casts=casts, next_ada=next_ada)
        xs = res[0]
        if not last:
            w_in, w_out, wi, wo2 = (_Layer(r[None], 0) for r in res[1:5])
            mod = res[5]
    return xs.reshape(b, s, d)
```

```python
import functools
from typing import NamedTuple

import numpy as np
import jax
import jax.numpy as jnp
from jax import lax
from jax.experimental import pallas as pl
from jax.experimental.pallas import tpu as pltpu

HEAD_DIM = 64
A_Q_HEADS = 16
A_KV_HEADS = 4
A_HALF_WINDOW = 128
B_GROUPS = ((128, 1), (512, 4), (2048, 16))
B_Q_HEADS = 8
B_KV_HEADS = 2
GQ = 4
RMS_EPS = 1e-6
NEG = -1e30
LOG2E = 1.4426950408889634
LN2 = 0.6931471805599453
Q_SCALE = HEAD_DIM ** -0.5 * LOG2E

LANES = 128
QBLK = 128
A_UNITS_PER_STEP, B_UNITS_PER_STEP = 64, 32
ONES_COLS = 32
HEADS_PER_PV = 4
A_PV_DELAY, B_PV_DELAY = 0, 1
FFN_SPLIT = 1536
QKV_ROW_SLICES = 2
POST_ROW_SLICES = 2
VMEM_LIMIT = 56 * 1024 * 1024

BF16 = jnp.bfloat16
F32 = jnp.float32


def _alibi_slopes(n):
    return np.asarray(2.0 ** (-8.0 * np.arange(1, n + 1) / n), dtype=np.float32)


class _Layer(NamedTuple):
    arr: jax.Array
    idx: int


def _layer_spec(p):
    nd = p.arr.ndim
    return pl.BlockSpec((None,) + p.arr.shape[1:], lambda *_: (p.idx,) + (0,) * (nd - 1),
                        pipeline_mode=pl.Buffered(1))


def _params(n_axes=1, **kwargs):
    return pltpu.CompilerParams(
        dimension_semantics=("arbitrary",) * n_axes, vmem_limit_bytes=VMEM_LIMIT, **kwargs)


def _ada_block(c_ref, w_ref, b_ref, o_ref):
    c = c_ref[...]
    cond = c * jax.nn.sigmoid(c)
    o_ref[...] = jnp.sum(w_ref[...] * cond, axis=0, keepdims=True) + b_ref[...]


def _ada_specs(ada_w, layer, tn, col_block):
    d, n = ada_w.shape[1:]
    per = d // tn
    assert d % tn == 0 and n % d == 0
    return ([pl.BlockSpec((d, 1), lambda *i: (0, 0)),
             pl.BlockSpec((None, d, tn), lambda *i: (layer, 0, col_block(*i))),
             pl.BlockSpec((None, 1, tn), lambda *i: (layer, 0, col_block(*i)))],
            pl.BlockSpec((None, 1, tn), lambda *i: (col_block(*i) // per, 0, col_block(*i) % per)),
            jax.ShapeDtypeStruct((n // d, 1, d), F32))


def _ada_mod(c_col, ada_w, ada_b3, layer):
    d, n = ada_w.shape[1:]
    in_specs, out_spec, out_shape = _ada_specs(ada_w, layer, d, lambda j: j)
    return pl.pallas_call(
        _ada_block,
        out_shape=out_shape,
        grid=(n // d,),
        in_specs=in_specs,
        out_specs=out_spec,
        compiler_params=_params(),
        name="ada_mod",
    )(c_col, ada_w, ada_b3)


def _norm_mod(x, g, shift, scale):
    ms = jnp.mean(x * x, axis=-1, keepdims=True)
    y = x * lax.rsqrt(ms + RMS_EPS)
    return y * (g * (1.0 + scale)) + shift


class _Cast(NamedTuple):
    arr: jax.Array
    idx: int
    cols: tuple | None = None


BF16_SUBLANES = 16


def _cast_specs(casts, steps):
    in_specs, args, out_shape, out_specs = [], [], [], []
    for cst in casts:
        _, n_rows, n_cols = cst.arr.shape
        n_blk = next(n for n in range(steps, 0, -1)
                     if n_rows % n == 0 and (n_rows // n) % BF16_SUBLANES == 0)
        rows, last = n_rows // n_blk, n_blk - 1
        in_specs.append(pl.BlockSpec((None, rows, n_cols),
                                     lambda i, cst=cst, last=last: (cst.idx, jnp.minimum(i, last), 0)))
        args.append(cst.arr)
        out_shape.append(jax.ShapeDtypeStruct((n_rows, n_cols), BF16))
        out_specs.append(pl.BlockSpec((rows, n_cols),
                                      lambda i, last=last: (jnp.minimum(i, last), 0)))
    return in_specs, args, out_shape, out_specs


def _run_casts(srcs, dsts, cast_cols):
    for src, dst, cols in zip(srcs, dsts, cast_cols):
        if cols is None:
            dst[...] = src[...].astype(BF16)
        else:
            at = 0
            for start, width in cols:
                dst[:, at:at + width] = src[:, start:start + width].astype(BF16)
                at += width


def _qkv_a_kernel(x_ref, mod_ref, g_ref, w_ref, *refs, cast_cols):
    n_cast = len(cast_cols)
    cast_src, (q_ref, kv_ref), cast_dst = refs[:n_cast], refs[n_cast:n_cast + 2], refs[n_cast + 2:]
    _run_casts(cast_src, cast_dst, cast_cols)
    dq = q_ref.shape[1]
    rows = x_ref.shape[0] // QKV_ROW_SLICES
    for n in range(QKV_ROW_SLICES):
        rs = slice(n * rows, (n + 1) * rows)
        h = _norm_mod(x_ref[rs, :], g_ref[...], mod_ref[0], mod_ref[1])
        r = jnp.dot(h.astype(BF16), w_ref[...], preferred_element_type=F32)
        q_ref[rs, :] = (r[:, :dq] * Q_SCALE).astype(BF16)
        kv_ref[rs, :] = r[:, dq:].astype(BF16)


def _qkv_a(x, mod, g, w, casts=()):
    s, d = x.shape
    dq = A_Q_HEADS * HEAD_DIM
    dkv = w.arr.shape[2] - dq
    tm = 1024
    cast_in, cast_args, cast_shape, cast_out = _cast_specs(casts, s // tm)
    return pl.pallas_call(
        functools.partial(_qkv_a_kernel, cast_cols=tuple(cst.cols for cst in casts)),
        out_shape=(jax.ShapeDtypeStruct((s, dq), BF16),
                   jax.ShapeDtypeStruct((s, dkv), BF16), *cast_shape),
        grid=(s // tm,),
        in_specs=[
            pl.BlockSpec((tm, d), lambda i: (i, 0)),
            _layer_spec(mod),
            _layer_spec(g),
            _layer_spec(w),
            *cast_in,
        ],
        out_specs=(pl.BlockSpec((tm, dq), lambda i: (i, 0)),
                   pl.BlockSpec((tm, dkv), lambda i: (i, 0)), *cast_out),
        compiler_params=_params(),
        name="qkv_a",
    )(x, mod.arr, g.arr, w.arr, *cast_args)


B_DQ = B_Q_HEADS * HEAD_DIM
B_DKV = 2 * B_KV_HEADS * HEAD_DIM
B_DG = B_DQ + B_DKV


def _qkv_b_kernel(x_ref, mod_ref, g_ref, w_ref, *refs):
    out_refs, slab_ref = refs[:-1], refs[-1]
    tm = x_ref.shape[0]
    h = _norm_mod(x_ref[...], g_ref[...], mod_ref[0], mod_ref[1])
    hb = h.astype(BF16)
    nq = B_DQ // LANES
    for gi, (_, dil) in enumerate(B_GROUPS):
        q_ref, kv_ref = out_refs[2 * gi], out_refs[2 * gi + 1]
        r = jnp.dot(hb, w_ref[:, gi * B_DG:(gi + 1) * B_DG], preferred_element_type=F32)
        rq = r[:, :B_DQ] * Q_SCALE
        rkv = r[:, B_DQ:]
        if dil == 1:
            q_ref[0] = rq.astype(BF16)
            kv_ref[0] = rkv.astype(BF16)
            continue
        for c in range(B_DG // LANES):
            src = rq if c < nq else rkv
            cc = c if c < nq else c - nq
            slab_ref[c] = src[:, cc * LANES:(cc + 1) * LANES]
        rows = tm // dil
        for rr in range(dil):
            for c in range(B_DG // LANES):
                v = slab_ref[c, pl.ds(rr, rows, stride=dil), :].astype(BF16)
                if c < nq:
                    q_ref[rr, :, c * LANES:(c + 1) * LANES] = v
                else:
                    kv_ref[rr, :, (c - nq) * LANES:(c - nq + 1) * LANES] = v


def _qkv_b(x, mod, g, w):
    s, d = x.shape
    tm = 1024
    out_shape, out_specs = [], []
    for _, dil in B_GROUPS:
        for width in (B_DQ, B_DKV):
            out_shape.append(jax.ShapeDtypeStruct((dil, s // dil, width), BF16))
            out_specs.append(pl.BlockSpec((dil, tm // dil, width), lambda i: (0, i, 0)))
    return pl.pallas_call(
        _qkv_b_kernel,
        out_shape=tuple(out_shape),
        grid=(s // tm,),
        in_specs=[
            pl.BlockSpec((tm, d), lambda i: (i, 0)),
            _layer_spec(mod),
            _layer_spec(g),
            _layer_spec(w),
        ],
        out_specs=tuple(out_specs),
        scratch_shapes=[pltpu.VMEM((B_DG // LANES, tm, LANES), F32)],
        compiler_params=_params(),
        name="qkv_b",
    )(x, mod.arr, g.arr, w.arr)


def _attn_units(q_ref, kv_refs, sink_ref, o_ref, lse_ref, bufs, bias_ref, tile, first_step,
                *, n_kv, halo, seq_len, dist_scale, slopes, sink_row):
    kvp_ref, kvc_ref, kvn_ref = kv_refs
    has_sink = sink_row is not None
    emit_lse = lse_ref is not None
    ts = q_ref.shape[0]
    w = QBLK + 2 * halo
    nh = n_kv * GQ

    @pl.when(first_step)
    def _():
        kj = lax.broadcasted_iota(jnp.int32, (w, QBLK), 0)
        qi = lax.broadcasted_iota(jnp.int32, (w, QBLK), 1)
        rel = jnp.abs(kj - halo - qi)
        dist = (dist_scale * rel).astype(F32)
        for h, slope in enumerate(slopes):
            bias_ref[h] = jnp.where(rel <= halo, ((-float(slope)) * dist) * LOG2E, NEG)
        bias_ref[nh] = jnp.full((w, QBLK), NEG, F32)

    kvbuf, vt_ref = bufs
    kvbuf[0:halo, :] = kvp_ref[...]
    kvbuf[halo:halo + ts, :] = kvc_ref[...]
    kvbuf[halo + ts:, :] = kvn_ref[...]
    dkv = n_kv * HEAD_DIM
    for pair in range(n_kv // 2):
        vt_ref[pair * LANES:(pair + 1) * LANES, :] = (
            kvbuf[:, dkv + pair * LANES:dkv + (pair + 1) * LANES].T)

    ones = jnp.ones((ONES_COLS, w), BF16)
    nt_dims = (((1,), (1,)), ((), ()))

    def scores(r0, kvh):
        c0 = kvh * GQ * HEAD_DIM
        qg = q_ref[pl.ds(r0, QBLK), c0:c0 + GQ * HEAD_DIM]
        qs = jnp.concatenate(
            [qg[:, g * HEAD_DIM:(g + 1) * HEAD_DIM] for g in range(GQ)], axis=0)
        k = kvbuf[pl.ds(r0, w), kvh * HEAD_DIM:(kvh + 1) * HEAD_DIM]
        return lax.dot_general(k, qs, nt_dims, preferred_element_type=F32)

    def softmax(st, r0, kvh, gs):
        seq_pos = (tile * ts + r0) & (seq_len - 1)
        prev_out = seq_pos == 0
        next_out = seq_pos + QBLK == seq_len
        ps, ms = [], []
        for g in gs:
            h = kvh * GQ + g
            top = jnp.where(prev_out, nh, h)
            bot = jnp.where(next_out, nh, h)
            bias = jnp.concatenate([bias_ref[top, 0:halo, :],
                                    bias_ref[h, halo:halo + QBLK, :],
                                    bias_ref[bot, halo + QBLK:w, :]], axis=0)
            sg = st[:, g * QBLK:(g + 1) * QBLK] + bias
            m = jnp.max(sg, axis=0, keepdims=True)
            if has_sink:
                m = jnp.maximum(m, sink_ref[sink_row, h] * LOG2E)
            ps.append(jnp.exp2(sg - m).astype(BF16))
            ms.append(m)
        return jnp.concatenate(ps, axis=1), ms

    def weighted_values(pt, ms, r0, kvh, gs):
        vt = vt_ref[kvh * HEAD_DIM:(kvh + 1) * HEAD_DIM, r0:r0 + w]
        vt_ext = jnp.concatenate([vt, ones], axis=0)
        ot = jnp.dot(vt_ext, pt, preferred_element_type=F32)
        outs, lses = [], []
        for n, g in enumerate(gs):
            h = kvh * GQ + g
            cs = slice(n * QBLK, (n + 1) * QBLK)
            l = ot[HEAD_DIM:HEAD_DIM + 1, cs]
            if has_sink:
                l = l + jnp.exp2(sink_ref[sink_row, h] * LOG2E - ms[n])
            outs.append(ot[0:HEAD_DIM, cs] * (1.0 / l))
            if emit_lse:
                lses.append(jnp.broadcast_to(ms[n] * LN2 + jnp.log(l), (HEAD_DIM, QBLK)))
        for n in range(0, len(gs), 2):
            c0 = (kvh * GQ + gs[n]) * HEAD_DIM
            o2 = jnp.concatenate(outs[n:n + 2], axis=0)
            o_ref[r0:r0 + QBLK, c0:c0 + LANES] = o2.T.astype(o_ref.dtype)
            if emit_lse:
                l2 = jnp.concatenate(lses[n:n + 2], axis=0)
                lse_ref[r0:r0 + QBLK, c0:c0 + LANES] = l2.T

    units = [(qb * QBLK, kvh) for qb in range(ts // QBLK) for kvh in range(n_kv)]
    head_sets = [tuple(range(i, i + HEADS_PER_PV)) for i in range(0, GQ, HEADS_PER_PV)]
    pending, probs = {}, {}

    def issue(u):
        pending[u] = scores(*units[u])

    def soft(u):
        st = pending.pop(u)
        probs[u] = [softmax(st, *units[u], gs) for gs in head_sets]

    def finish(u):
        for gs, (pt, ms) in zip(head_sets, probs.pop(u)):
            weighted_values(pt, ms, *units[u], gs)

    return [(functools.partial(issue, u), functools.partial(soft, u), functools.partial(finish, u))
            for u in range(len(units))]


def _attn_kernel(*refs, emit_lse, pv_delay, **cfg):
    refs = list(refs)
    q_ref, kv_refs = refs[0], refs[1:4]
    pos = 4
    sink_ref = None
    if cfg["sink_row"] is not None:
        sink_ref = refs[pos]
        pos += 1
    o_ref = refs[pos]
    pos += 1
    lse_ref = None
    if emit_lse:
        lse_ref = refs[pos]
        pos += 1
    kvbufs, bias_ref = refs[pos:pos + 2], refs[pos + 2]
    step = pl.program_id(0)
    units = _attn_units(q_ref, kv_refs, sink_ref, o_ref, lse_ref, kvbufs, bias_ref,
                        step, step == 0, **cfg)
    _run_units(units, pv_delay)


def _run_units(units, pv_delay):
    units[0][0]()
    for u, (_, soft, _) in enumerate(units):
        if u + 1 < len(units):
            units[u + 1][0]()
        soft()
        if u >= pv_delay:
            units[u - pv_delay][2]()
    for u in range(max(len(units) - pv_delay, 0), len(units)):
        units[u][2]()


def _attn_in_specs(s, ts, dq, dkv, halo, tile_of=lambda i: i):
    per = ts // halo
    last_halo = s // halo - 1
    return [
        pl.BlockSpec((ts, dq), lambda i: (tile_of(i), 0)),
        pl.BlockSpec((halo, dkv), lambda i: (jnp.maximum(tile_of(i) * per - 1, 0), 0)),
        pl.BlockSpec((ts, dkv), lambda i: (tile_of(i), 0)),
        pl.BlockSpec((halo, dkv), lambda i: (jnp.minimum((tile_of(i) + 1) * per, last_halo), 0)),
    ]


def _attn_scratch(ts, dkv, n_kv, halo):
    return [pltpu.VMEM((ts + 2 * halo, dkv), BF16),
            pltpu.VMEM((n_kv * HEAD_DIM, ts + 2 * halo), BF16),
            pltpu.VMEM((n_kv * GQ + 1, QBLK + 2 * halo, QBLK), F32)]


def _attention(q, kv, *, n_kv, halo, seq_len, dist_scale, slopes, pv_delay, units_per_step,
               sink=None, emit_lse=False, out_dtype=BF16):
    s, dq = q.shape
    dkv = kv.shape[1]
    ts = QBLK * (units_per_step // n_kv)
    assert s % ts == 0 and (ts % seq_len == 0 or seq_len % ts == 0)
    assert ts % halo == 0 and dq == n_kv * GQ * HEAD_DIM
    assert dkv == 2 * n_kv * HEAD_DIM
    in_specs = _attn_in_specs(s, ts, dq, dkv, halo)
    args = [q, kv, kv, kv]
    if sink is not None:
        in_specs.append(pl.BlockSpec(memory_space=pltpu.SMEM))
        args.append(sink.arr)
    out_shape = [jax.ShapeDtypeStruct((s, dq), out_dtype)]
    out_specs = [pl.BlockSpec((ts, dq), lambda i: (i, 0))]
    if emit_lse:
        out_shape.append(jax.ShapeDtypeStruct((s, dq), F32))
        out_specs.append(pl.BlockSpec((ts, dq), lambda i: (i, 0)))
    kern = functools.partial(
        _attn_kernel, emit_lse=emit_lse, pv_delay=pv_delay, n_kv=n_kv, halo=halo, seq_len=seq_len,
        dist_scale=dist_scale, slopes=tuple(float(x) for x in slopes),
        sink_row=None if sink is None else sink.idx)
    return pl.pallas_call(
        kern,
        out_shape=tuple(out_shape),
        grid=(s // ts,),
        in_specs=in_specs,
        out_specs=tuple(out_specs),
        scratch_shapes=_attn_scratch(ts, dkv, n_kv, halo),
        compiler_params=_params(),
        name="banded_attn",
    )(*args)


def _attn_groups_kernel(*refs, cfgs, pv_delay, tiles):
    ng = len(cfgs)
    in_refs = refs[:4 * ng]
    o_ref, lse_ref = refs[4 * ng:4 * ng + 2]
    kvbufs, bias_ref = refs[4 * ng + 2:4 * ng + 4], refs[4 * ng + 4]
    step = pl.program_id(0)
    tile = step % tiles
    for gi, cfg in enumerate(cfgs):
        @pl.when(step // tiles == gi)
        def _(gi=gi, cfg=cfg):
            q_ref, *kv_refs = in_refs[4 * gi:4 * gi + 4]
            units = _attn_units(q_ref, kv_refs, None, o_ref, lse_ref, kvbufs, bias_ref,
                                tile, tile == 0, sink_row=None, **cfg)
            _run_units(units, pv_delay)


def _attention_groups(qs, kvs, cfgs, *, n_kv, halo, pv_delay, units_per_step):
    ng = len(qs)
    s, dq = qs[0].shape
    dkv = kvs[0].shape[1]
    ts = QBLK * (units_per_step // n_kv)
    tiles = s // ts
    assert s % ts == 0 and ts % halo == 0 and dq == n_kv * GQ * HEAD_DIM
    assert dkv == 2 * n_kv * HEAD_DIM
    in_specs, args = [], []
    for gi in range(ng):
        tile_of = lambda i, gi=gi: jnp.clip(i - gi * tiles, 0, tiles - 1)
        in_specs += _attn_in_specs(s, ts, dq, dkv, halo, tile_of)
        args += [qs[gi], kvs[gi], kvs[gi], kvs[gi]]
        assert ts % cfgs[gi]["seq_len"] == 0 or cfgs[gi]["seq_len"] % ts == 0
    out_spec = pl.BlockSpec((None, ts, dq), lambda i: (i // tiles, i % tiles, 0))
    kern = functools.partial(
        _attn_groups_kernel, pv_delay=pv_delay, tiles=tiles,
        cfgs=tuple(dict(cfg, n_kv=n_kv, halo=halo) for cfg in cfgs))
    return pl.pallas_call(
        kern,
        out_shape=(jax.ShapeDtypeStruct((ng, s, dq), F32),) * 2,
        grid=(ng * tiles,),
        in_specs=in_specs,
        out_specs=(out_spec, out_spec),
        scratch_shapes=_attn_scratch(ts, dkv, n_kv, halo),
        compiler_params=_params(),
        name="banded_attn_groups",
    )(*args)


def _silu(x):
    return x * jax.nn.sigmoid(x)


def _post_tail(get_o, x_ref, mod_ref, g_ref, wo_ref, wi_ref, wo2_ref, fin_ref, out_ref):
    tm = x_ref.shape[0]
    rows = tm // POST_ROW_SLICES
    dff = wo2_ref.shape[0]
    chunks = ((0, FFN_SPLIT), (FFN_SPLIT, dff))

    def gate_up(st, c):
        lo, hi = chunks[c]
        st["gu"] = (jnp.dot(st["h2"], wi_ref[:, lo:hi], preferred_element_type=F32),
                    jnp.dot(st["h2"], wi_ref[:, dff + lo:dff + hi], preferred_element_type=F32))

    def down(st, c):
        lo, hi = chunks[c]
        gate, up = st["gu_prev"]
        act = (_silu(gate) * up).astype(BF16)
        part = jnp.dot(act, wo2_ref[lo:hi, :], preferred_element_type=F32)
        st["acc"] = part if c == 0 else st["acc"] + part

    def stage(st, k):
        if k == 0:
            st["y"] = jnp.dot(get_o(st["rs"]), wo_ref[...], preferred_element_type=F32)
        elif k == 1:
            st["x1"] = x_ref[st["rs"], :] + mod_ref[2] * st["y"]
            st["h2"] = _norm_mod(st["x1"], g_ref[...], mod_ref[3],
                                 mod_ref[4]).astype(BF16)
            gate_up(st, 0)
        elif k <= len(chunks):
            st["gu_prev"] = st["gu"]
            gate_up(st, k - 1)
            down(st, k - 2)
        else:
            st["gu_prev"] = st["gu"]
            down(st, len(chunks) - 1)
            x2 = st["x1"] + mod_ref[5] * st["acc"]
            if fin_ref is not None:
                ms = jnp.mean(x2 * x2, axis=-1, keepdims=True)
                x2 = (x2 * lax.rsqrt(ms + RMS_EPS)) * fin_ref[...]
            out_ref[st["rs"], :] = x2

    states = [{"rs": slice(n * rows, (n + 1) * rows)} for n in range(POST_ROW_SLICES)]
    for k in range(len(chunks) + 2):
        for st in states:
            stage(st, k)


def _merged_o(attn_refs, slabs, om_ref):
    o0_ref, l0_ref, o1_ref, l1_ref, o2_ref, l2_ref = attn_refs
    so1, sl1, so2, sl2 = slabs
    nc = B_DQ // LANES

    def merged(rs):
        for src_o, src_l, dst_o, dst_l in ((o1_ref, l1_ref, so1, sl1), (o2_ref, l2_ref, so2, sl2)):
            dil = src_o.shape[0]
            n_src = (rs.stop - rs.start) // dil
            src_rs = slice(rs.start // dil, rs.start // dil + n_src)
            for rr in range(dil):
                dst_rows = pl.ds(rs.start + rr, n_src, stride=dil)
                for c in range(nc):
                    cs = slice(c * LANES, (c + 1) * LANES)
                    dst_o[c, dst_rows, :] = src_o[rr, src_rs, cs]
                    dst_l[c, dst_rows, :] = src_l[rr, src_rs, cs]
        for c in range(nc):
            cs = slice(c * LANES, (c + 1) * LANES)
            l0, l1, l2 = l0_ref[0, rs, cs], sl1[c, rs, :], sl2[c, rs, :]
            mx = jnp.maximum(jnp.maximum(l0, l1), l2)
            w0, w1, w2 = jnp.exp(l0 - mx), jnp.exp(l1 - mx), jnp.exp(l2 - mx)
            num = w0 * o0_ref[0, rs, cs] + w1 * so1[c, rs, :] + w2 * so2[c, rs, :]
            om_ref[rs, cs] = (num / (w0 + w1 + w2)).astype(BF16)
        return om_ref[rs, :]

    return merged


def _post_kernel(*refs, n_attn, final, cast_cols, has_ada):
    refs = list(refs)
    attn_refs = refs[:n_attn]
    x_ref, mod_ref, g_ref, wo_ref, wi_ref, wo2_ref = refs[n_attn:n_attn + 6]
    pos = n_attn + 6
    fin_ref = None
    if final:
        fin_ref = refs[pos]
        pos += 1
    n_cast = len(cast_cols)
    cast_src = refs[pos:pos + n_cast]
    pos += n_cast
    ada_in = refs[pos:pos + 3] if has_ada else None
    pos += 3 if has_ada else 0
    out_ref = refs[pos]
    cast_dst = refs[pos + 1:pos + 1 + n_cast]
    pos += 1 + n_cast
    ada_out = refs[pos] if has_ada else None
    pos += 1 if has_ada else 0
    scratch = refs[pos:]

    _run_casts(cast_src, cast_dst, cast_cols)
    if has_ada:
        _ada_block(*ada_in, ada_out)

    if n_attn == 1:
        (o_ref,) = attn_refs
        get_o = lambda rs: o_ref[rs, :]
    else:
        get_o = _merged_o(attn_refs, scratch[:4], scratch[4])
    _post_tail(get_o, x_ref, mod_ref, g_ref, wo_ref, wi_ref, wo2_ref, fin_ref, out_ref)


def _post(o_parts, x, mod, wo, g, wi, wo2, final_g, *, casts=(), next_ada=None):
    s, d = x.shape
    tm = 512
    steps = s // tm
    final = final_g is not None
    dilated = len(o_parts) > 1
    in_specs, args, scratch = [], [], []
    if dilated:
        for arr, gi, dil in o_parts:
            in_specs.append(pl.BlockSpec((dil, tm // dil, arr.shape[2]),
                                         lambda i, gi=gi: (gi, i, 0)))
            args.append(arr)
        scratch = [pltpu.VMEM((B_DQ // LANES, tm, LANES), F32) for _ in range(4)]
        scratch.append(pltpu.VMEM((tm, B_DQ), BF16))
    else:
        (o,) = o_parts
        in_specs.append(pl.BlockSpec((tm, o.shape[1]), lambda i: (i, 0)))
        args.append(o)
    params = [mod, g, wo, wi, wo2] + ([final_g] if final else [])
    in_specs += [pl.BlockSpec((tm, d), lambda i: (i, 0))] + [_layer_spec(p) for p in params]
    args += [x] + [p.arr for p in params]
    out_shape = [jax.ShapeDtypeStruct((s, d), F32)]
    out_specs = [pl.BlockSpec((tm, d), lambda i: (i, 0))]
    cast_in, cast_args, cast_shape, cast_out = _cast_specs(casts, steps)
    in_specs += cast_in
    args += cast_args
    out_shape += cast_shape
    out_specs += cast_out
    if next_ada is not None:
        c_col, ada_w, ada_b3, layer = next_ada
        n = ada_w.shape[2]
        n_blk = next(k for k in range(steps, 0, -1) if n % k == 0 and (n // k) % LANES == 0)
        tn, last = n // n_blk, n_blk - 1
        ada_in, ada_out, ada_shape = _ada_specs(ada_w, layer, tn,
                                                lambda i: jnp.minimum(i, last))
        in_specs += ada_in
        args += [c_col, ada_w, ada_b3]
        out_shape.append(ada_shape)
        out_specs.append(ada_out)
    kern = functools.partial(
        _post_kernel, n_attn=len(o_parts), final=final,
        cast_cols=tuple(cst.cols for cst in casts), has_ada=next_ada is not None)
    return pl.pallas_call(
        kern,
        out_shape=tuple(out_shape),
        grid=(steps,),
        in_specs=in_specs,
        out_specs=tuple(out_specs),
        scratch_shapes=scratch,
        compiler_params=_params(),
        name="post_b" if dilated else "post_a",
    )(*args)


def _b_w_in_cols():
    ng = len(B_GROUPS)
    dq, dk = ng * B_DQ, ng * B_KV_HEADS * HEAD_DIM
    dkg = B_KV_HEADS * HEAD_DIM
    cols = []
    for gi in range(ng):
        cols += [(gi * B_DQ, B_DQ), (dq + gi * dkg, dkg), (dq + dk + gi * dkg, dkg)]
    return tuple(cols)


def kernel(x, c, ada_w, ada_b, norm_mix, norm_ffn, ffn_w_in, ffn_w_out,
           a_w_in, a_w_out, a_sink, b_w_in, b_w_out, final_norm):
    b, s, d = x.shape
    assert b == 1
    depth = ada_w.shape[0]
    xs = x.reshape(s, d)
    c_col = c.reshape(d, 1)
    ada_b3 = ada_b.reshape(depth, 1, -1)
    g_mix = norm_mix.reshape(depth, 1, d)
    g_ffn = norm_ffn.reshape(depth, 1, d)
    slopes_a = _alibi_slopes(A_Q_HEADS)
    slopes_b = _alibi_slopes(len(B_GROUPS) * B_Q_HEADS)
    b_cols = _b_w_in_cols()

    def mixer_params(i):
        return (a_w_in, a_w_out, i // 2, None) if i % 2 == 0 else (b_w_in, b_w_out, i // 2, b_cols)

    mod = _ada_mod(c_col, ada_w, ada_b3, 0)
    w_in = _Layer(a_w_in[0].astype(BF16)[None], 0)
    casts0 = (_Cast(a_w_out, 0), _Cast(ffn_w_in, 0), _Cast(ffn_w_out, 0))
    for i in range(depth):
        mod_l = _Layer(mod[None], 0)
        if i % 2 == 0:
            q, kv, *cast0 = _qkv_a(xs, mod_l, _Layer(g_mix, i), w_in, casts0 if i == 0 else ())
            if i == 0:
                w_out, wi, wo2 = (_Layer(r[None], 0) for r in cast0)
            o_parts = _attention(q, kv, n_kv=A_KV_HEADS, halo=A_HALF_WINDOW, seq_len=s,
                                 dist_scale=1, slopes=slopes_a, pv_delay=A_PV_DELAY,
                                 units_per_step=A_UNITS_PER_STEP,
                                 sink=_Layer(a_sink, i // 2))
        else:
            outs = _qkv_b(xs, mod_l, _Layer(g_mix, i), w_in)
            (halo,) = {window // (2 * dil) for window, dil in B_GROUPS}
            cfgs = [dict(seq_len=s // dil, dist_scale=dil,
                         slopes=tuple(float(v) for v in
                                      slopes_b[gi * B_Q_HEADS:(gi + 1) * B_Q_HEADS]))
                    for gi, (_, dil) in enumerate(B_GROUPS)]
            og, lg = _attention_groups(
                [outs[2 * gi].reshape(s, B_DQ) for gi in range(len(B_GROUPS))],
                [outs[2 * gi + 1].reshape(s, B_DKV) for gi in range(len(B_GROUPS))],
                cfgs, n_kv=B_KV_HEADS, halo=halo, pv_delay=B_PV_DELAY,
                units_per_step=B_UNITS_PER_STEP)
            o_parts = []
            for gi, (_, dil) in enumerate(B_GROUPS):
                o_parts += [(arr.reshape(len(B_GROUPS) * dil, s // dil, B_DQ), gi, dil)
                            for arr in (og, lg)]
        last = i == depth - 1
        casts, next_ada = (), None
        if not last:
            m_in, m_out, j, cols = mixer_params(i + 1)
            casts = (_Cast(m_in, j, cols), _Cast(m_out, j),
                     _Cast(ffn_w_in, i + 1), _Cast(ffn_w_out, i + 1))
            next_ada = (c_col, ada_w, ada_b3, i + 1)
        fin = _Layer(final_norm.reshape(1, 1, d), 0) if last else None
        res = _post(o_parts, xs, mod_l, w_out, _Layer(g_ffn, i), wi, wo2, fin,
                    casts=casts, next_ada=next_ada)
        xs = res[0]
        if not last:
            w_in, w_out, wi, wo2 = (_Layer(r[None], 0) for r in res[1:5])
            mod = res[5]
    return xs.reshape(b, s, d)
```

```python
import functools
from typing import NamedTuple

import numpy as np
import jax
import jax.numpy as jnp
from jax import lax
from jax.experimental import pallas as pl
from jax.experimental.pallas import tpu as pltpu

HEAD_DIM = 64
A_Q_HEADS = 16
A_KV_HEADS = 4
A_HALF_WINDOW = 128
B_GROUPS = ((128, 1), (512, 4), (2048, 16))
B_Q_HEADS = 8
B_KV_HEADS = 2
GQ = 4
RMS_EPS = 1e-6
NEG = -1e30
LOG2E = 1.4426950408889634
LN2 = 0.6931471805599453
Q_SCALE = HEAD_DIM ** -0.5 * LOG2E

LANES = 128
QBLK = 128
A_UNITS_PER_STEP, B_UNITS_PER_STEP = 64, 64
ONES_COLS = 32
HEADS_PER_PV = 4
A_PV_DELAY, B_PV_DELAY = 0, 1
FFN_SPLIT = 1536
QKV_ROW_SLICES = 2
POST_ROW_SLICES = 2
VMEM_LIMIT = 56 * 1024 * 1024

BF16 = jnp.bfloat16
F32 = jnp.float32


def _alibi_slopes(n):
    return np.asarray(2.0 ** (-8.0 * np.arange(1, n + 1) / n), dtype=np.float32)


class _Layer(NamedTuple):
    arr: jax.Array
    idx: int


def _layer_spec(p):
    nd = p.arr.ndim
    return pl.BlockSpec((None,) + p.arr.shape[1:], lambda *_: (p.idx,) + (0,) * (nd - 1),
                        pipeline_mode=pl.Buffered(1))


def _params(n_axes=1, **kwargs):
    return pltpu.CompilerParams(
        dimension_semantics=("arbitrary",) * n_axes, vmem_limit_bytes=VMEM_LIMIT, **kwargs)


def _ada_block(c_ref, w_ref, b_ref, o_ref):
    c = c_ref[...]
    cond = c * jax.nn.sigmoid(c)
    o_ref[...] = jnp.sum(w_ref[...] * cond, axis=0, keepdims=True) + b_ref[...]


def _ada_specs(ada_w, layer, tn, col_block):
    d, n = ada_w.shape[1:]
    per = d // tn
    assert d % tn == 0 and n % d == 0
    return ([pl.BlockSpec((d, 1), lambda *i: (0, 0)),
             pl.BlockSpec((None, d, tn), lambda *i: (layer, 0, col_block(*i))),
             pl.BlockSpec((None, 1, tn), lambda *i: (layer, 0, col_block(*i)))],
            pl.BlockSpec((None, 1, tn), lambda *i: (col_block(*i) // per, 0, col_block(*i) % per)),
            jax.ShapeDtypeStruct((n // d, 1, d), F32))


def _ada_mod_kernel(c_ref, w_ref, b_ref, *refs, cast_cols):
    n = len(cast_cols)
    _run_casts(refs[:n], refs[n + 1:], cast_cols)
    _ada_block(c_ref, w_ref, b_ref, refs[n])


def _ada_mod(c_col, ada_w, ada_b3, layer, casts=()):
    d, n = ada_w.shape[1:]
    steps = n // d
    in_specs, out_spec, out_shape = _ada_specs(ada_w, layer, d, lambda j: j)
    cast_in, cast_args, cast_shape, cast_out = _cast_specs(casts, steps)
    return pl.pallas_call(
        functools.partial(_ada_mod_kernel, cast_cols=tuple(cst.cols for cst in casts)),
        out_shape=(out_shape, *cast_shape),
        grid=(steps,),
        in_specs=in_specs + cast_in,
        out_specs=(out_spec, *cast_out),
        compiler_params=_params(),
        name="ada_mod",
    )(c_col, ada_w, ada_b3, *cast_args)


def _norm_mod(x, g, shift, scale):
    ms = jnp.mean(x * x, axis=-1, keepdims=True)
    y = x * lax.rsqrt(ms + RMS_EPS)
    return y * (g * (1.0 + scale)) + shift


class _Cast(NamedTuple):
    arr: jax.Array
    idx: int
    cols: tuple | None = None


BF16_SUBLANES = 16


def _cast_specs(casts, steps):
    in_specs, args, out_shape, out_specs = [], [], [], []
    for cst in casts:
        _, n_rows, n_cols = cst.arr.shape
        n_blk = next(n for n in range(steps, 0, -1)
                     if n_rows % n == 0 and (n_rows // n) % BF16_SUBLANES == 0)
        rows, last = n_rows // n_blk, n_blk - 1
        in_specs.append(pl.BlockSpec((None, rows, n_cols),
                                     lambda i, cst=cst, last=last: (cst.idx, jnp.minimum(i, last), 0)))
        args.append(cst.arr)
        out_shape.append(jax.ShapeDtypeStruct((n_rows, n_cols), BF16))
        out_specs.append(pl.BlockSpec((rows, n_cols),
                                      lambda i, last=last: (jnp.minimum(i, last), 0)))
    return in_specs, args, out_shape, out_specs


def _run_casts(srcs, dsts, cast_cols):
    for src, dst, cols in zip(srcs, dsts, cast_cols):
        if cols is None:
            dst[...] = src[...].astype(BF16)
        else:
            at = 0
            for start, width in cols:
                dst[:, at:at + width] = src[:, start:start + width].astype(BF16)
                at += width


def _qkv_a_kernel(x_ref, mod_ref, g_ref, w_ref, *refs, cast_cols):
    n_cast = len(cast_cols)
    cast_src, (q_ref, kv_ref), cast_dst = refs[:n_cast], refs[n_cast:n_cast + 2], refs[n_cast + 2:]
    _run_casts(cast_src, cast_dst, cast_cols)
    dq = q_ref.shape[1]
    rows = x_ref.shape[0] // QKV_ROW_SLICES
    for n in range(QKV_ROW_SLICES):
        rs = slice(n * rows, (n + 1) * rows)
        h = _norm_mod(x_ref[rs, :], g_ref[...], mod_ref[0], mod_ref[1])
        r = jnp.dot(h.astype(BF16), w_ref[...], preferred_element_type=F32)
        q_ref[rs, :] = (r[:, :dq] * Q_SCALE).astype(BF16)
        kv_ref[rs, :] = r[:, dq:].astype(BF16)


def _qkv_a(x, mod, g, w, casts=()):
    s, d = x.shape
    dq = A_Q_HEADS * HEAD_DIM
    dkv = w.arr.shape[2] - dq
    tm = 1024
    cast_in, cast_args, cast_shape, cast_out = _cast_specs(casts, s // tm)
    return pl.pallas_call(
        functools.partial(_qkv_a_kernel, cast_cols=tuple(cst.cols for cst in casts)),
        out_shape=(jax.ShapeDtypeStruct((s, dq), BF16),
                   jax.ShapeDtypeStruct((s, dkv), BF16), *cast_shape),
        grid=(s // tm,),
        in_specs=[
            pl.BlockSpec((tm, d), lambda i: (i, 0)),
            _layer_spec(mod),
            _layer_spec(g),
            _layer_spec(w),
            *cast_in,
        ],
        out_specs=(pl.BlockSpec((tm, dq), lambda i: (i, 0)),
                   pl.BlockSpec((tm, dkv), lambda i: (i, 0)), *cast_out),
        compiler_params=_params(),
        name="qkv_a",
    )(x, mod.arr, g.arr, w.arr, *cast_args)


B_DQ = B_Q_HEADS * HEAD_DIM
B_DKV = 2 * B_KV_HEADS * HEAD_DIM
B_DG = B_DQ + B_DKV


def _qkv_b_kernel(x_ref, mod_ref, g_ref, w_ref, *refs):
    out_refs, slab_ref = refs[:-1], refs[-1]
    tm = x_ref.shape[0]
    h = _norm_mod(x_ref[...], g_ref[...], mod_ref[0], mod_ref[1])
    hb = h.astype(BF16)
    nq = B_DQ // LANES
    for gi, (_, dil) in enumerate(B_GROUPS):
        q_ref, kv_ref = out_refs[2 * gi], out_refs[2 * gi + 1]
        r = jnp.dot(hb, w_ref[:, gi * B_DG:(gi + 1) * B_DG], preferred_element_type=F32)
        rq = r[:, :B_DQ] * Q_SCALE
        rkv = r[:, B_DQ:]
        if dil == 1:
            q_ref[0] = rq.astype(BF16)
            kv_ref[0] = rkv.astype(BF16)
            continue
        for c in range(B_DG // LANES):
            src = rq if c < nq else rkv
            cc = c if c < nq else c - nq
            slab_ref[c] = src[:, cc * LANES:(cc + 1) * LANES]
        rows = tm // dil
        for rr in range(dil):
            for c in range(B_DG // LANES):
                v = slab_ref[c, pl.ds(rr, rows, stride=dil), :].astype(BF16)
                if c < nq:
                    q_ref[rr, :, c * LANES:(c + 1) * LANES] = v
                else:
                    kv_ref[rr, :, (c - nq) * LANES:(c - nq + 1) * LANES] = v


def _qkv_b(x, mod, g, w):
    s, d = x.shape
    tm = 1024
    out_shape, out_specs = [], []
    for _, dil in B_GROUPS:
        for width in (B_DQ, B_DKV):
            out_shape.append(jax.ShapeDtypeStruct((dil, s // dil, width), BF16))
            out_specs.append(pl.BlockSpec((dil, tm // dil, width), lambda i: (0, i, 0)))
    return pl.pallas_call(
        _qkv_b_kernel,
        out_shape=tuple(out_shape),
        grid=(s // tm,),
        in_specs=[
            pl.BlockSpec((tm, d), lambda i: (i, 0)),
            _layer_spec(mod),
            _layer_spec(g),
            _layer_spec(w),
        ],
        out_specs=tuple(out_specs),
        scratch_shapes=[pltpu.VMEM((B_DG // LANES, tm, LANES), F32)],
        compiler_params=_params(),
        name="qkv_b",
    )(x, mod.arr, g.arr, w.arr)


def _attn_units(q_ref, kv_refs, sink_ref, o_ref, lse_ref, bufs, bias_ref, tile, first_step,
                *, n_kv, halo, seq_len, dist_scale, slopes, sink_row):
    kvp_ref, kvc_ref, kvn_ref = kv_refs
    has_sink = sink_row is not None
    emit_lse = lse_ref is not None
    ts = q_ref.shape[0]
    w = QBLK + 2 * halo
    nh = n_kv * GQ

    @pl.when(first_step)
    def _():
        kj = lax.broadcasted_iota(jnp.int32, (w, QBLK), 0)
        qi = lax.broadcasted_iota(jnp.int32, (w, QBLK), 1)
        rel = jnp.abs(kj - halo - qi)
        dist = (dist_scale * rel).astype(F32)
        for h, slope in enumerate(slopes):
            bias_ref[h] = jnp.where(rel <= halo, ((-float(slope)) * dist) * LOG2E, NEG)
        bias_ref[nh] = jnp.full((w, QBLK), NEG, F32)

    kvbuf, vt_ref = bufs
    kvbuf[0:halo, :] = kvp_ref[...]
    kvbuf[halo:halo + ts, :] = kvc_ref[...]
    kvbuf[halo + ts:, :] = kvn_ref[...]
    dkv = n_kv * HEAD_DIM
    for pair in range(n_kv // 2):
        vt_ref[pair * LANES:(pair + 1) * LANES, :] = (
            kvbuf[:, dkv + pair * LANES:dkv + (pair + 1) * LANES].T)

    ones = jnp.ones((ONES_COLS, w), BF16)
    nt_dims = (((1,), (1,)), ((), ()))

    def scores(r0, kvh):
        c0 = kvh * GQ * HEAD_DIM
        qg = q_ref[pl.ds(r0, QBLK), c0:c0 + GQ * HEAD_DIM]
        qs = jnp.concatenate(
            [qg[:, g * HEAD_DIM:(g + 1) * HEAD_DIM] for g in range(GQ)], axis=0)
        k = kvbuf[pl.ds(r0, w), kvh * HEAD_DIM:(kvh + 1) * HEAD_DIM]
        return lax.dot_general(k, qs, nt_dims, preferred_element_type=F32)

    def softmax(st, r0, kvh, gs):
        seq_pos = (tile * ts + r0) & (seq_len - 1)
        prev_out = seq_pos == 0
        next_out = seq_pos + QBLK == seq_len
        ps, ms = [], []
        for g in gs:
            h = kvh * GQ + g
            top = jnp.where(prev_out, nh, h)
            bot = jnp.where(next_out, nh, h)
            bias = jnp.concatenate([bias_ref[top, 0:halo, :],
                                    bias_ref[h, halo:halo + QBLK, :],
                                    bias_ref[bot, halo + QBLK:w, :]], axis=0)
            sg = st[:, g * QBLK:(g + 1) * QBLK] + bias
            m = jnp.max(sg, axis=0, keepdims=True)
            if has_sink:
                m = jnp.maximum(m, sink_ref[sink_row, h] * LOG2E)
            ps.append(jnp.exp2(sg - m).astype(BF16))
            ms.append(m)
        return jnp.concatenate(ps, axis=1), ms

    def weighted_values(pt, ms, r0, kvh, gs):
        vt = vt_ref[kvh * HEAD_DIM:(kvh + 1) * HEAD_DIM, r0:r0 + w]
        vt_ext = jnp.concatenate([vt, ones], axis=0)
        ot = jnp.dot(vt_ext, pt, preferred_element_type=F32)
        outs, lses = [], []
        for n, g in enumerate(gs):
            h = kvh * GQ + g
            cs = slice(n * QBLK, (n + 1) * QBLK)
            l = ot[HEAD_DIM:HEAD_DIM + 1, cs]
            if has_sink:
                l = l + jnp.exp2(sink_ref[sink_row, h] * LOG2E - ms[n])
            outs.append(ot[0:HEAD_DIM, cs] * (1.0 / l))
            if emit_lse:
                lses.append(jnp.broadcast_to(ms[n] * LN2 + jnp.log(l), (HEAD_DIM, QBLK)))
        for n in range(0, len(gs), 2):
            c0 = (kvh * GQ + gs[n]) * HEAD_DIM
            o2 = jnp.concatenate(outs[n:n + 2], axis=0)
            o_ref[r0:r0 + QBLK, c0:c0 + LANES] = o2.T.astype(o_ref.dtype)
            if emit_lse:
                l2 = jnp.concatenate(lses[n:n + 2], axis=0)
                lse_ref[r0:r0 + QBLK, c0:c0 + LANES] = l2.T

    units = [(qb * QBLK, kvh) for qb in range(ts // QBLK) for kvh in range(n_kv)]
    head_sets = [tuple(range(i, i + HEADS_PER_PV)) for i in range(0, GQ, HEADS_PER_PV)]
    pending, probs = {}, {}

    def issue(u):
        pending[u] = scores(*units[u])

    def soft(u):
        st = pending.pop(u)
        probs[u] = [softmax(st, *units[u], gs) for gs in head_sets]

    def finish(u):
        for gs, (pt, ms) in zip(head_sets, probs.pop(u)):
            weighted_values(pt, ms, *units[u], gs)

    return [(functools.partial(issue, u), functools.partial(soft, u), functools.partial(finish, u))
            for u in range(len(units))]


def _attn_kernel(*refs, emit_lse, pv_delay, **cfg):
    refs = list(refs)
    q_ref, kv_refs = refs[0], refs[1:4]
    pos = 4
    sink_ref = None
    if cfg["sink_row"] is not None:
        sink_ref = refs[pos]
        pos += 1
    o_ref = refs[pos]
    pos += 1
    lse_ref = None
    if emit_lse:
        lse_ref = refs[pos]
        pos += 1
    kvbufs, bias_ref = refs[pos:pos + 2], refs[pos + 2]
    step = pl.program_id(0)
    units = _attn_units(q_ref, kv_refs, sink_ref, o_ref, lse_ref, kvbufs, bias_ref,
                        step, step == 0, **cfg)
    units[0][0]()
    for u, (_, soft, _) in enumerate(units):
        if u + 1 < len(units):
            units[u + 1][0]()
        soft()
        if u >= pv_delay:
            units[u - pv_delay][2]()
    for u in range(max(len(units) - pv_delay, 0), len(units)):
        units[u][2]()


def _attn_in_specs(s, ts, dq, dkv, halo):
    per = ts // halo
    last_halo = s // halo - 1
    return [
        pl.BlockSpec((ts, dq), lambda i: (i, 0)),
        pl.BlockSpec((halo, dkv), lambda i: (jnp.maximum(i * per - 1, 0), 0)),
        pl.BlockSpec((ts, dkv), lambda i: (i, 0)),
        pl.BlockSpec((halo, dkv), lambda i: (jnp.minimum((i + 1) * per, last_halo), 0)),
    ]


def _attn_scratch(ts, dkv, n_kv, halo):
    return [pltpu.VMEM((ts + 2 * halo, dkv), BF16),
            pltpu.VMEM((n_kv * HEAD_DIM, ts + 2 * halo), BF16),
            pltpu.VMEM((n_kv * GQ + 1, QBLK + 2 * halo, QBLK), F32)]


def _attention(q, kv, *, n_kv, halo, seq_len, dist_scale, slopes, pv_delay, units_per_step,
               sink=None, emit_lse=False, out_dtype=BF16):
    s, dq = q.shape
    dkv = kv.shape[1]
    ts = QBLK * (units_per_step // n_kv)
    assert s % ts == 0 and (ts % seq_len == 0 or seq_len % ts == 0)
    assert ts % halo == 0 and dq == n_kv * GQ * HEAD_DIM
    assert dkv == 2 * n_kv * HEAD_DIM
    in_specs = _attn_in_specs(s, ts, dq, dkv, halo)
    args = [q, kv, kv, kv]
    if sink is not None:
        in_specs.append(pl.BlockSpec(memory_space=pltpu.SMEM))
        args.append(sink.arr)
    out_shape = [jax.ShapeDtypeStruct((s, dq), out_dtype)]
    out_specs = [pl.BlockSpec((ts, dq), lambda i: (i, 0))]
    if emit_lse:
        out_shape.append(jax.ShapeDtypeStruct((s, dq), F32))
        out_specs.append(pl.BlockSpec((ts, dq), lambda i: (i, 0)))
    kern = functools.partial(
        _attn_kernel, emit_lse=emit_lse, pv_delay=pv_delay, n_kv=n_kv, halo=halo, seq_len=seq_len,
        dist_scale=dist_scale, slopes=tuple(float(x) for x in slopes),
        sink_row=None if sink is None else sink.idx)
    return pl.pallas_call(
        kern,
        out_shape=tuple(out_shape),
        grid=(s // ts,),
        in_specs=in_specs,
        out_specs=tuple(out_specs),
        scratch_shapes=_attn_scratch(ts, dkv, n_kv, halo),
        compiler_params=_params(),
        name="banded_attn",
    )(*args)


def _silu(x):
    return x * jax.nn.sigmoid(x)


def _post_tail(get_o, x_ref, mod_ref, g_ref, wo_ref, wi_ref, wo2_ref, fin_ref, out_ref):
    tm = x_ref.shape[0]
    rows = tm // POST_ROW_SLICES
    dff = wo2_ref.shape[0]
    chunks = ((0, FFN_SPLIT), (FFN_SPLIT, dff))

    def gate_up(st, c):
        lo, hi = chunks[c]
        st["gu"] = (jnp.dot(st["h2"], wi_ref[:, lo:hi], preferred_element_type=F32),
                    jnp.dot(st["h2"], wi_ref[:, dff + lo:dff + hi], preferred_element_type=F32))

    def down(st, c):
        lo, hi = chunks[c]
        gate, up = st["gu_prev"]
        act = (_silu(gate) * up).astype(BF16)
        part = jnp.dot(act, wo2_ref[lo:hi, :], preferred_element_type=F32)
        st["acc"] = part if c == 0 else st["acc"] + part

    def stage(st, k):
        if k == 0:
            st["y"] = jnp.dot(get_o(st["rs"]), wo_ref[...], preferred_element_type=F32)
        elif k == 1:
            st["x1"] = x_ref[st["rs"], :] + mod_ref[2] * st["y"]
            st["h2"] = _norm_mod(st["x1"], g_ref[...], mod_ref[3],
                                 mod_ref[4]).astype(BF16)
            gate_up(st, 0)
        elif k <= len(chunks):
            st["gu_prev"] = st["gu"]
            gate_up(st, k - 1)
            down(st, k - 2)
        else:
            st["gu_prev"] = st["gu"]
            down(st, len(chunks) - 1)
            x2 = st["x1"] + mod_ref[5] * st["acc"]
            if fin_ref is not None:
                ms = jnp.mean(x2 * x2, axis=-1, keepdims=True)
                x2 = (x2 * lax.rsqrt(ms + RMS_EPS)) * fin_ref[...]
            out_ref[st["rs"], :] = x2

    states = [{"rs": slice(n * rows, (n + 1) * rows)} for n in range(POST_ROW_SLICES)]
    for k in range(len(chunks) + 2):
        for st in states:
            stage(st, k)


def _merged_o(attn_refs, slabs, om_ref):
    o0_ref, l0_ref, o1_ref, l1_ref, o2_ref, l2_ref = attn_refs
    so1, sl1, so2, sl2 = slabs
    nc = B_DQ // LANES

    def merged(rs):
        for src_o, src_l, dst_o, dst_l in ((o1_ref, l1_ref, so1, sl1), (o2_ref, l2_ref, so2, sl2)):
            dil = src_o.shape[0]
            n_src = (rs.stop - rs.start) // dil
            src_rs = slice(rs.start // dil, rs.start // dil + n_src)
            for rr in range(dil):
                dst_rows = pl.ds(rs.start + rr, n_src, stride=dil)
                for c in range(nc):
                    cs = slice(c * LANES, (c + 1) * LANES)
                    dst_o[c, dst_rows, :] = src_o[rr, src_rs, cs]
                    dst_l[c, dst_rows, :] = src_l[rr, src_rs, cs]
        for c in range(nc):
            cs = slice(c * LANES, (c + 1) * LANES)
            l0, l1, l2 = l0_ref[0, rs, cs], sl1[c, rs, :], sl2[c, rs, :]
            mx = jnp.maximum(jnp.maximum(l0, l1), l2)
            w0, w1, w2 = jnp.exp(l0 - mx), jnp.exp(l1 - mx), jnp.exp(l2 - mx)
            num = w0 * o0_ref[0, rs, cs] + w1 * so1[c, rs, :] + w2 * so2[c, rs, :]
            om_ref[rs, cs] = (num / (w0 + w1 + w2)).astype(BF16)
        return om_ref[rs, :]

    return merged


def _post_kernel(*refs, n_attn, final, cast_cols, has_ada):
    refs = list(refs)
    attn_refs = refs[:n_attn]
    x_ref, mod_ref, g_ref, wo_ref, wi_ref, wo2_ref = refs[n_attn:n_attn + 6]
    pos = n_attn + 6
    fin_ref = None
    if final:
        fin_ref = refs[pos]
        pos += 1
    n_cast = len(cast_cols)
    cast_src = refs[pos:pos + n_cast]
    pos += n_cast
    ada_in = refs[pos:pos + 3] if has_ada else None
    pos += 3 if has_ada else 0
    out_ref = refs[pos]
    cast_dst = refs[pos + 1:pos + 1 + n_cast]
    pos += 1 + n_cast
    ada_out = refs[pos] if has_ada else None
    pos += 1 if has_ada else 0
    scratch = refs[pos:]

    _run_casts(cast_src, cast_dst, cast_cols)
    if has_ada:
        _ada_block(*ada_in, ada_out)

    if n_attn == 1:
        (o_ref,) = attn_refs
        get_o = lambda rs: o_ref[rs, :]
    else:
        get_o = _merged_o(attn_refs, scratch[:4], scratch[4])
    _post_tail(get_o, x_ref, mod_ref, g_ref, wo_ref, wi_ref, wo2_ref, fin_ref, out_ref)


def _post(o_parts, x, mod, wo, g, wi, wo2, final_g, *, casts=(), next_ada=None):
    s, d = x.shape
    tm = 512
    steps = s // tm
    final = final_g is not None
    dilated = len(o_parts) > 1
    in_specs, args, scratch = [], [], []
    if dilated:
        for arr in o_parts:
            dil = arr.shape[0]
            in_specs.append(pl.BlockSpec((dil, tm // dil, arr.shape[2]), lambda i: (0, i, 0)))
            args.append(arr)
        scratch = [pltpu.VMEM((B_DQ // LANES, tm, LANES), F32) for _ in range(4)]
        scratch.append(pltpu.VMEM((tm, B_DQ), BF16))
    else:
        (o,) = o_parts
        in_specs.append(pl.BlockSpec((tm, o.shape[1]), lambda i: (i, 0)))
        args.append(o)
    params = [mod, g, wo, wi, wo2] + ([final_g] if final else [])
    in_specs += [pl.BlockSpec((tm, d), lambda i: (i, 0))] + [_layer_spec(p) for p in params]
    args += [x] + [p.arr for p in params]
    out_shape = [jax.ShapeDtypeStruct((s, d), F32)]
    out_specs = [pl.BlockSpec((tm, d), lambda i: (i, 0))]
    cast_in, cast_args, cast_shape, cast_out = _cast_specs(casts, steps)
    in_specs += cast_in
    args += cast_args
    out_shape += cast_shape
    out_specs += cast_out
    if next_ada is not None:
        c_col, ada_w, ada_b3, layer = next_ada
        n = ada_w.shape[2]
        n_blk = next(k for k in range(steps, 0, -1) if n % k == 0 and (n // k) % LANES == 0)
        tn, last = n // n_blk, n_blk - 1
        ada_in, ada_out, ada_shape = _ada_specs(ada_w, layer, tn,
                                                lambda i: jnp.minimum(i, last))
        in_specs += ada_in
        args += [c_col, ada_w, ada_b3]
        out_shape.append(ada_shape)
        out_specs.append(ada_out)
    kern = functools.partial(
        _post_kernel, n_attn=len(o_parts), final=final,
        cast_cols=tuple(cst.cols for cst in casts), has_ada=next_ada is not None)
    return pl.pallas_call(
        kern,
        out_shape=tuple(out_shape),
        grid=(steps,),
        in_specs=in_specs,
        out_specs=tuple(out_specs),
        scratch_shapes=scratch,
        compiler_params=_params(),
        name="post_b" if dilated else "post_a",
    )(*args)


def _b_w_in_cols():
    ng = len(B_GROUPS)
    dq, dk = ng * B_DQ, ng * B_KV_HEADS * HEAD_DIM
    dkg = B_KV_HEADS * HEAD_DIM
    cols = []
    for gi in range(ng):
        cols += [(gi * B_DQ, B_DQ), (dq + gi * dkg, dkg), (dq + dk + gi * dkg, dkg)]
    return tuple(cols)


def kernel(x, c, ada_w, ada_b, norm_mix, norm_ffn, ffn_w_in, ffn_w_out,
           a_w_in, a_w_out, a_sink, b_w_in, b_w_out, final_norm):
    b, s, d = x.shape
    assert b == 1
    depth = ada_w.shape[0]
    xs = x.reshape(s, d)
    c_col = c.reshape(d, 1)
    ada_b3 = ada_b.reshape(depth, 1, -1)
    g_mix = norm_mix.reshape(depth, 1, d)
    g_ffn = norm_ffn.reshape(depth, 1, d)
    slopes_a = _alibi_slopes(A_Q_HEADS)
    slopes_b = _alibi_slopes(len(B_GROUPS) * B_Q_HEADS)
    b_cols = _b_w_in_cols()

    def mixer_params(i):
        return (a_w_in, a_w_out, i // 2, None) if i % 2 == 0 else (b_w_in, b_w_out, i // 2, b_cols)

    mod, w_in0 = _ada_mod(c_col, ada_w, ada_b3, 0, (_Cast(a_w_in, 0),))
    w_in = _Layer(w_in0[None], 0)
    casts0 = (_Cast(a_w_out, 0), _Cast(ffn_w_in, 0), _Cast(ffn_w_out, 0))
    for i in range(depth):
        mod_l = _Layer(mod[None], 0)
        if i % 2 == 0:
            q, kv, *cast0 = _qkv_a(xs, mod_l, _Layer(g_mix, i), w_in, casts0 if i == 0 else ())
            if i == 0:
                w_out, wi, wo2 = (_Layer(r[None], 0) for r in cast0)
            o_parts = _attention(q, kv, n_kv=A_KV_HEADS, halo=A_HALF_WINDOW, seq_len=s,
                                 dist_scale=1, slopes=slopes_a, pv_delay=A_PV_DELAY,
                                 units_per_step=A_UNITS_PER_STEP,
                                 sink=_Layer(a_sink, i // 2))
        else:
            outs = _qkv_b(xs, mod_l, _Layer(g_mix, i), w_in)
            o_parts = []
            for gi, (window, dil) in enumerate(B_GROUPS):
                qg = outs[2 * gi].reshape(s, B_DQ)
                kvg = outs[2 * gi + 1].reshape(s, B_DKV)
                og, lg = _attention(
                    qg, kvg, n_kv=B_KV_HEADS, halo=window // (2 * dil), seq_len=s // dil,
                    dist_scale=dil, slopes=slopes_b[gi * B_Q_HEADS:(gi + 1) * B_Q_HEADS],
                    pv_delay=B_PV_DELAY, units_per_step=B_UNITS_PER_STEP,
                    emit_lse=True, out_dtype=F32)
                o_parts += [og.reshape(dil, s // dil, B_DQ), lg.reshape(dil, s // dil, B_DQ)]
        last = i == depth - 1
        casts, next_ada = (), None
        if not last:
            m_in, m_out, j, cols = mixer_params(i + 1)
            casts = (_Cast(m_in, j, cols), _Cast(m_out, j),
                     _Cast(ffn_w_in, i + 1), _Cast(ffn_w_out, i + 1))
            next_ada = (c_col, ada_w, ada_b3, i + 1)
        fin = _Layer(final_norm.reshape(1, 1, d), 0) if last else None
        res = _post(o_parts, xs, mod_l, w_out, _Layer(g_ffn, i), wi, wo2, fin,
                    casts=casts, next_ada=next_ada)
        xs = res[0]
        if not last:
            w_in, w_out, wi, wo2 = (_Layer(r[None], 0) for r in res[1:5])
            mod = res[5]
    return xs.reshape(b, s, d)
```
